```python
import math
import jax, jax.numpy as jnp
from jax import lax
import numpy as np

D_MODEL = 2048
BATCH = 4
SEQ = 4096
DEPTH = 4

N_MIXERS = 3
HEAD_DIM = 128
N_HEADS = D_MODEL // HEAD_DIM
ROPE_THETA = 10000.0
D_FF = 4 * D_MODEL
EPS = 1e-6
SB_Q_BLOCK = 128
S5_GROUP = 16
S5_GROUPS = D_MODEL // S5_GROUP
S5_STATE = 64
S5_CHUNK = 128
S5_DT_MIN = 0.001
S5_DT_MAX = 0.1
IDX_HEADS = 16
IDX_DIM = 64
DSA_TOPK_MAX = 256
DSA_Q_BLOCK = 32
DSA_IN = 3 * D_MODEL + IDX_HEADS * IDX_DIM + IDX_DIM + IDX_HEADS

kernel_name = "hybrid_sb_s5_dsa_trunk"


def rmsnorm(x, g):
    xf = x.astype(jnp.float32)
    y = xf * lax.rsqrt(jnp.mean(xf * xf, axis=-1, keepdims=True) + EPS)
    return (y * g.astype(jnp.float32)).astype(x.dtype)


def rope(x, positions):
    d = x.shape[-1]
    inv_freq = ROPE_THETA ** (-jnp.arange(0, d, 2, dtype=jnp.float32) / d)
    ang = positions.astype(jnp.float32)[..., None] * inv_freq
    cos = jnp.cos(ang)[:, :, None, :]
    sin = jnp.sin(ang)[:, :, None, :]
    xf = x.astype(jnp.float32)
    x1, x2 = xf[..., : d // 2], xf[..., d // 2:]
    return jnp.concatenate([x1 * cos - x2 * sin, x2 * cos + x1 * sin], axis=-1).astype(x.dtype)


def modulate(h, shift, scale):
    return h * (1 + scale[:, None, :]) + shift[:, None, :]


def sq_relu_mlp(h, w1, w2):
    a = jax.nn.relu(h @ w1)
    return (a * a) @ w2


def stick_breaking_mixer(h, w_in, q_gain, k_gain, w_out):
    B, S, _ = h.shape
    q, k, v = jnp.split(h @ w_in, 3, axis=-1)
    q = rmsnorm(q.reshape(B, S, N_HEADS, HEAD_DIM), q_gain)
    k = rmsnorm(k.reshape(B, S, N_HEADS, HEAD_DIM), k_gain)
    v = v.reshape(B, S, N_HEADS, HEAD_DIM)
    nb = S // SB_Q_BLOCK
    qb = q.reshape(B, nb, SB_Q_BLOCK, N_HEADS, HEAD_DIM).transpose(1, 0, 3, 2, 4)
    kpos = jnp.arange(S)
    scale = HEAD_DIM ** -0.5

    def block(args):
        qblk, start = args
        qpos = start + jnp.arange(SB_Q_BLOCK)
        z = jnp.einsum('bhtd,bshd->bhts', qblk, k, preferred_element_type=jnp.float32) * scale
        causal = kpos[None, :] < qpos[:, None]
        log_not = jnp.where(causal, jax.nn.log_sigmoid(-z), 0.0)
        later = lax.cumsum(log_not, axis=3, reverse=True) - log_not
        w = jnp.where(causal, jnp.exp(jax.nn.log_sigmoid(z) + later), 0.0)
        return jnp.einsum('bhts,bshd->bthd', w.astype(v.dtype), v)

    starts = jnp.arange(nb) * SB_Q_BLOCK
    o = lax.map(block, (qb, starts))
    o = o.transpose(1, 0, 2, 3, 4).reshape(B, S, D_MODEL)
    return o @ w_out


def s5_mixer(h, w_in, lam_re, lam_im, log_dt, b_re, b_im, c_re, c_im, d_skip, w_glu):
    B, S, _ = h.shape
    u = h @ w_in
    ug = u.astype(jnp.float32).reshape(B, S, S5_GROUPS, S5_GROUP)
    dt = jnp.exp(log_dt.astype(jnp.float32))[:, None]
    lr = lam_re.astype(jnp.float32)
    li = lam_im.astype(jnp.float32)
    mag = jnp.exp(lr * dt)
    ar = mag * jnp.cos(li * dt)
    ai = mag * jnp.sin(li * dt)
    den = lr * lr + li * li
    fr = ((ar - 1.0) * lr + ai * li) / den
    fi = (ai * lr - (ar - 1.0) * li) / den
    br_ = b_re.astype(jnp.float32)
    bi_ = b_im.astype(jnp.float32)
    bbr = fr[..., None] * br_ - fi[..., None] * bi_
    bbi = fr[..., None] * bi_ + fi[..., None] * br_
    cr = c_re.astype(jnp.float32)
    ci = c_im.astype(jnp.float32)

    def combine(e1, e2):
        a1r, a1i, b1r, b1i = e1
        a2r, a2i, b2r, b2i = e2
        return (a1r * a2r - a1i * a2i, a1r * a2i + a1i * a2r,
                a2r * b1r - a2i * b1i + b2r, a2r * b1i + a2i * b1r + b2i)

    nc = S // S5_CHUNK
    uc = ug.reshape(B, nc, S5_CHUNK, S5_GROUPS, S5_GROUP).transpose(1, 0, 2, 3, 4)

    def chunk_step(carry, u_chunk):
        hr0, hi0 = carry
        bur = jnp.einsum('btgc,gpc->btgp', u_chunk, bbr)
        bui = jnp.einsum('btgc,gpc->btgp', u_chunk, bbi)
        a_r = jnp.broadcast_to(ar, bur.shape)
        a_i = jnp.broadcast_to(ai, bur.shape)
        pr, pi, sr, si = lax.associative_scan(combine, (a_r, a_i, bur, bui), axis=1)
        hr = sr + pr * hr0[:, None] - pi * hi0[:, None]
        hi = si + pr * hi0[:, None] + pi * hr0[:, None]
        y = jnp.einsum('btgp,gcp->btgc', hr, cr) - jnp.einsum('btgp,gcp->btgc', hi, ci)
        return (hr[:, -1], hi[:, -1]), y

    h0 = jnp.zeros((B, S5_GROUPS, S5_STATE), jnp.float32)
    _, ys = lax.scan(chunk_step, (h0, h0), uc)
    y = ys.transpose(1, 0, 2, 3, 4).reshape(B, S, D_MODEL)
    y = y + d_skip.astype(jnp.float32) * u.astype(jnp.float32)
    z = jax.nn.gelu(y).astype(h.dtype)
    a, g = jnp.split(z @ w_glu, 2, axis=-1)
    return a * jax.nn.sigmoid(g)


def dsa_mixer(h, positions, w_in, q_gain, k_gain, w_out):
    B, S, _ = h.shape
    D = D_MODEL
    cuts = [D, 2 * D, 3 * D, 3 * D + IDX_HEADS * IDX_DIM, 3 * D + IDX_HEADS * IDX_DIM + IDX_DIM]
    q, k, v, qi, ki, wi = jnp.split(h @ w_in, cuts, axis=-1)
    q = rope(rmsnorm(q.reshape(B, S, N_HEADS, HEAD_DIM), q_gain), positions)
    k = rope(rmsnorm(k.reshape(B, S, N_HEADS, HEAD_DIM), k_gain), positions)
    v = v.reshape(B, S, N_HEADS, HEAD_DIM)
    qi = rope(qi.reshape(B, S, IDX_HEADS, IDX_DIM), positions)
    ki = rope(ki.reshape(B, S, 1, IDX_DIM), positions)[:, :, 0]
    wi = wi * IDX_HEADS ** -0.5
    topk = min(DSA_TOPK_MAX, S // 4)
    nb = S // DSA_Q_BLOCK
    qb = q.reshape(B, nb, DSA_Q_BLOCK, N_HEADS, HEAD_DIM).transpose(1, 0, 2, 3, 4)
    qib = qi.reshape(B, nb, DSA_Q_BLOCK, IDX_HEADS, IDX_DIM).transpose(1, 0, 2, 3, 4)
    wib = wi.reshape(B, nb, DSA_Q_BLOCK, IDX_HEADS).transpose(1, 0, 2, 3)
    kpos = jnp.arange(S)
    gather = jax.vmap(lambda arr, idx: arr[idx])

    def block(args):
        qblk, qiblk, wiblk, start = args
        qpos = start + jnp.arange(DSA_Q_BLOCK)
        rel = jnp.einsum('bthd,bsd->bths', qiblk, ki, preferred_element_type=jnp.float32) * IDX_DIM ** -0.5
        score = jnp.einsum('bth,bths->bts', wiblk.astype(jnp.float32), jax.nn.relu(rel))
        causal = kpos[None, :] <= qpos[:, None]
        score = jnp.where(causal[None], score, -jnp.inf)
        _, idx = lax.top_k(score, topk)
        valid = idx <= qpos[None, :, None]
        k_sel = gather(k, idx)
        v_sel = gather(v, idx)
        logits = jnp.einsum('bthd,btkhd->bhtk', qblk, k_sel, preferred_element_type=jnp.float32) * HEAD_DIM ** -0.5
        logits = jnp.where(valid[:, None], logits, -jnp.inf)
        p = jax.nn.softmax(logits, axis=-1)
        return jnp.einsum('bhtk,btkhd->bthd', p.astype(v.dtype), v_sel)

    starts = jnp.arange(nb) * DSA_Q_BLOCK
    o = lax.map(block, (qb, qib, wib, starts))
    o = o.transpose(1, 0, 2, 3, 4).reshape(B, S, D_MODEL)
    return o @ w_out


def setup_inputs(seed: int = 0) -> dict:
    key = jax.random.key(seed)
    ks = iter(jax.random.split(key, 32))
    f32 = jnp.float32

    def nrm(shape, scale):
        return jax.random.normal(next(ks), shape, f32) * scale

    D = D_MODEL
    n_sb = len(range(0, DEPTH, N_MIXERS))
    n_s5 = len(range(1, DEPTH, N_MIXERS))
    n_dsa = len(range(2, DEPTH, N_MIXERS))
    G, P, Gc = S5_GROUPS, S5_STATE, S5_GROUP
    x = nrm((BATCH, SEQ, D), 1.0)
    c = nrm((BATCH, D), 1.0)
    positions = jnp.tile(jnp.arange(SEQ, dtype=jnp.int32)[None, :], (BATCH, 1))
    ln1_g = 1.0 + nrm((DEPTH, D), 0.01)
    ln2_g = 1.0 + nrm((DEPTH, D), 0.01)
    ada_w = nrm((DEPTH, D, 6 * D), 0.5 * D ** -0.5)
    ada_b = nrm((DEPTH, 6 * D), 0.02)
    mlp_w1 = nrm((DEPTH, D, D_FF), D ** -0.5)
    mlp_w2 = nrm((DEPTH, D_FF, D), D_FF ** -0.5)
    sb_w_in = nrm((n_sb, D, 3 * D), D ** -0.5)
    sb_q_gain = 1.0 + nrm((n_sb, HEAD_DIM), 0.01)
    sb_k_gain = 1.0 + nrm((n_sb, HEAD_DIM), 0.01)
    sb_w_out = nrm((n_sb, D, D), D ** -0.5)
    s5_w_in = nrm((n_s5, D, D), D ** -0.5)
    s5_lambda_re = -0.5 + nrm((n_s5, G, P), 0.01)
    s5_lambda_im = jnp.pi * jnp.arange(P, dtype=f32)[None, None, :] + nrm((n_s5, G, P), 0.01)
    s5_log_dt = jax.random.uniform(next(ks), (n_s5, G), f32, math.log(S5_DT_MIN), math.log(S5_DT_MAX))
    s5_b_re = nrm((n_s5, G, P, Gc), Gc ** -0.5)
    s5_b_im = nrm((n_s5, G, P, Gc), Gc ** -0.5)
    s5_c_re = nrm((n_s5, G, Gc, P), P ** -0.5)
    s5_c_im = nrm((n_s5, G, Gc, P), P ** -0.5)
    s5_d = nrm((n_s5, D), 1.0)
    s5_w_glu = nrm((n_s5, D, 2 * D), D ** -0.5)
    dsa_w_in = nrm((n_dsa, D, DSA_IN), D ** -0.5)
    dsa_q_gain = 1.0 + nrm((n_dsa, HEAD_DIM), 0.01)
    dsa_k_gain = 1.0 + nrm((n_dsa, HEAD_DIM), 0.01)
    dsa_w_out = nrm((n_dsa, D, D), D ** -0.5)
    return {"x": x, "c": c, "positions": positions,
            "ln1_g": ln1_g, "ln2_g": ln2_g, "ada_w": ada_w, "ada_b": ada_b,
            "mlp_w1": mlp_w1, "mlp_w2": mlp_w2,
            "sb_w_in": sb_w_in, "sb_q_gain": sb_q_gain, "sb_k_gain": sb_k_gain, "sb_w_out": sb_w_out,
            "s5_w_in": s5_w_in, "s5_lambda_re": s5_lambda_re, "s5_lambda_im": s5_lambda_im,
            "s5_log_dt": s5_log_dt, "s5_b_re": s5_b_re, "s5_b_im": s5_b_im,
            "s5_c_re": s5_c_re, "s5_c_im": s5_c_im, "s5_d": s5_d, "s5_w_glu": s5_w_glu,
            "dsa_w_in": dsa_w_in, "dsa_q_gain": dsa_q_gain, "dsa_k_gain": dsa_k_gain, "dsa_w_out": dsa_w_out}


def reference(x, c, positions, ln1_g, ln2_g, ada_w, ada_b, mlp_w1, mlp_w2,
              sb_w_in, sb_q_gain, sb_k_gain, sb_w_out,
              s5_w_in, s5_lambda_re, s5_lambda_im, s5_log_dt, s5_b_re, s5_b_im,
              s5_c_re, s5_c_im, s5_d, s5_w_glu,
              dsa_w_in, dsa_q_gain, dsa_k_gain, dsa_w_out):
    cond = jax.nn.silu(c)
    counts = [0, 0, 0]
    for i in range(DEPTH):
        mod = cond @ ada_w[i] + ada_b[i]
        sh1, sc1, g1, sh2, sc2, g2 = jnp.split(mod, 6, axis=-1)
        h = modulate(rmsnorm(x, ln1_g[i]), sh1, sc1)
        kind = i % N_MIXERS
        j = counts[kind]
        counts[kind] += 1
        if kind == 0:
            y = stick_breaking_mixer(h, sb_w_in[j], sb_q_gain[j], sb_k_gain[j], sb_w_out[j])
        elif kind == 1:
            y = s5_mixer(h, s5_w_in[j], s5_lambda_re[j], s5_lambda_im[j], s5_log_dt[j],
                         s5_b_re[j], s5_b_im[j], s5_c_re[j], s5_c_im[j], s5_d[j], s5_w_glu[j])
        else:
            y = dsa_mixer(h, positions, dsa_w_in[j], dsa_q_gain[j], dsa_k_gain[j], dsa_w_out[j])
        x = x + g1[:, None, :] * y
        h = modulate(rmsnorm(x, ln2_g[i]), sh2, sc2)
        x = x + g2[:, None, :] * sq_relu_mlp(h, mlp_w1[i], mlp_w2[i])
    return x
```

```python
import functools
import math

import jax
import jax.numpy as jnp
from jax import lax
from jax.experimental import pallas as pl
from jax.experimental.pallas import tpu as pltpu

F32 = jnp.float32
BF16 = jnp.bfloat16

N_MIXERS = 3
HEAD_DIM = 128
ROPE_THETA = 10000.0
EPS = 1e-6
S5_GROUP = 16
S5_STATE = 64
IDX_HEADS = 16
IDX_DIM = 64
DSA_TOPK_MAX = 256

LANES = 128
SUBLANES = 8
VMEM_LIMIT_BYTES = 56 * 1024 * 1024

S5_CHUNK = 16
NEG_BIG = -1e30
INT_MIN = -(2 ** 31)


def _cparams(sem):
    return pltpu.CompilerParams(dimension_semantics=sem, vmem_limit_bytes=VMEM_LIMIT_BYTES)


def _sigmoid(x):
    return 1.0 / (1.0 + jnp.exp(-x))


def _gelu_tanh(x):
    c = math.sqrt(2.0 / math.pi)
    return 0.5 * x * (1.0 + jnp.tanh(c * (x + 0.044715 * (x * x * x))))


def _rms_mod(x_ref, g_ref, sh_ref, sc_ref):
    x = x_ref[...]
    ms = jnp.mean(x * x, axis=-1, keepdims=True)
    y = x * lax.rsqrt(ms + EPS) * g_ref[...]
    return y * (1.0 + sc_ref[...]) + sh_ref[...]


def _head_rmsnorm(y, gain):
    ms = jnp.mean(y * y, axis=-1, keepdims=True)
    return y * lax.rsqrt(ms + EPS) * gain


def _ada_kernel(c_ref, w_ref, b_ref, o_ref):
    c = c_ref[...]
    cond = (c * _sigmoid(c)).astype(BF16)
    o_ref[...] = jnp.dot(cond, w_ref[...].astype(BF16), preferred_element_type=F32) + b_ref[...]


def _ada_mod(c, ada_w, ada_b):
    depth, d, n = ada_w.shape
    b = c.shape[0]
    c8 = jnp.zeros((SUBLANES, d), F32).at[:b].set(c)
    tn = 1024
    return pl.pallas_call(
        _ada_kernel,
        grid=(depth, n // tn),
        in_specs=[
            pl.BlockSpec((SUBLANES, d), lambda l, j: (0, 0)),
            pl.BlockSpec((None, d, tn), lambda l, j: (l, 0, j)),
            pl.BlockSpec((None, 1, tn), lambda l, j: (l, 0, j)),
        ],
        out_specs=pl.BlockSpec((None, SUBLANES, tn), lambda l, j: (l, 0, j)),
        out_shape=jax.ShapeDtypeStruct((depth, SUBLANES, n), F32),
        compiler_params=_cparams(("parallel", "parallel")),
        name="ada_mod",
    )(c8, ada_w, ada_b.reshape(depth, 1, n))


def _mod_specs(d, shift_chunk, scale_chunk):
    return [
        pl.BlockSpec((None, 1, d), lambda b, i, j: (b, 0, shift_chunk)),
        pl.BlockSpec((None, 1, d), lambda b, i, j: (b, 0, scale_chunk)),
    ]


def _nm_matmul_kernel(x_ref, g_ref, sh_ref, sc_ref, w_ref, o_ref, h_ref):
    @pl.when(pl.program_id(2) == 0)
    def _():
        h_ref[...] = _rms_mod(x_ref, g_ref, sh_ref, sc_ref).astype(BF16)

    o_ref[...] = jnp.dot(h_ref[...], w_ref[...], preferred_element_type=F32).astype(o_ref.dtype)


def _nm_matmul(x, ln_g, mod3, w, out_dtype, tm=1024, tn=1024):
    b, s, d = x.shape
    n = w.shape[1]
    return pl.pallas_call(
        _nm_matmul_kernel,
        grid=(b, s // tm, n // tn),
        in_specs=[
            pl.BlockSpec((None, tm, d), lambda b, i, j: (b, i, 0)),
            pl.BlockSpec((1, d), lambda b, i, j: (0, 0)),
            *_mod_specs(d, 0, 1),
            pl.BlockSpec((d, tn), lambda b, i, j: (0, j)),
        ],
        out_specs=pl.BlockSpec((None, tm, tn), lambda b, i, j: (b, i, j)),
        out_shape=jax.ShapeDtypeStruct((b, s, n), out_dtype),
        scratch_shapes=[pltpu.VMEM((tm, d), BF16)],
        compiler_params=_cparams(("parallel", "parallel", "arbitrary")),
        name="nm_matmul",
    )(x, ln_g.reshape(1, d), mod3, mod3, w)


def _sb_inproj_kernel(x_ref, g_ref, sh_ref, sc_ref, w_ref, gain_ref, o_ref, h_ref, *, n_norm_tiles):
    j = pl.program_id(2)

    @pl.when(j == 0)
    def _():
        h_ref[...] = _rms_mod(x_ref, g_ref, sh_ref, sc_ref).astype(BF16)

    y = jnp.dot(h_ref[...], w_ref[...], preferred_element_type=F32)

    @pl.when(j < n_norm_tiles)
    def _():
        for hh in range(y.shape[1] // HEAD_DIM):
            sl = slice(hh * HEAD_DIM, (hh + 1) * HEAD_DIM)
            o_ref[:, sl] = _head_rmsnorm(y[:, sl], gain_ref[:, sl]).astype(o_ref.dtype)

    @pl.when(j >= n_norm_tiles)
    def _():
        o_ref[...] = y.astype(o_ref.dtype)


def _sb_inproj(x, ln_g, mod3, w, gain_row, tm=1024, tn=1024):
    b, s, d = x.shape
    n = w.shape[1]
    return pl.pallas_call(
        functools.partial(_sb_inproj_kernel, n_norm_tiles=2 * d // tn),
        grid=(b, s // tm, n // tn),
        in_specs=[
            pl.BlockSpec((None, tm, d), lambda b, i, j: (b, i, 0)),
            pl.BlockSpec((1, d), lambda b, i, j: (0, 0)),
            *_mod_specs(d, 0, 1),
            pl.BlockSpec((d, tn), lambda b, i, j: (0, j)),
            pl.BlockSpec((1, tn), lambda b, i, j: (0, j)),
        ],
        out_specs=pl.BlockSpec((None, tm, tn), lambda b, i, j: (b, i, j)),
        out_shape=jax.ShapeDtypeStruct((b, s, n), BF16),
        scratch_shapes=[pltpu.VMEM((tm, d), BF16)],
        compiler_params=_cparams(("parallel", "parallel", "arbitrary")),
        name="sb_inproj",
    )(x, ln_g.reshape(1, d), mod3, mod3, w, gain_row)


def _out_gate_kernel(a_ref, w_ref, x_ref, gate_ref, o_ref):
    y = jnp.dot(a_ref[...], w_ref[...], preferred_element_type=F32)
    o_ref[...] = x_ref[...] + gate_ref[...] * y


def _out_gate(a, w, x, mod3, gate_chunk, tm=1024, tn=1024):
    b, s, d = x.shape
    k = a.shape[2]
    nt = d // tn
    return pl.pallas_call(
        _out_gate_kernel,
        grid=(b, s // tm, nt),
        in_specs=[
            pl.BlockSpec((None, tm, k), lambda b, i, j: (b, i, 0)),
            pl.BlockSpec((k, tn), lambda b, i, j: (0, j)),
            pl.BlockSpec((None, tm, tn), lambda b, i, j: (b, i, j)),
            pl.BlockSpec((None, 1, tn), lambda b, i, j: (b, 0, gate_chunk * nt + j)),
        ],
        out_specs=pl.BlockSpec((None, tm, tn), lambda b, i, j: (b, i, j)),
        out_shape=jax.ShapeDtypeStruct((b, s, d), F32),
        compiler_params=_cparams(("parallel", "parallel", "parallel")),
        name="out_gate",
    )(a, w, x, mod3)


def _glu_gate_kernel(z_ref, wa_ref, wg_ref, x_ref, gate_ref, o_ref):
    z = z_ref[...]
    a = jnp.dot(z, wa_ref[...], preferred_element_type=F32)
    g = jnp.dot(z, wg_ref[...], preferred_element_type=F32)
    o_ref[...] = x_ref[...] + gate_ref[...] * (a * _sigmoid(g))


def _glu_gate(z, w_glu, x, mod3, gate_chunk, tm=1024, tn=512):
    b, s, d = x.shape
    nt = d // tn
    return pl.pallas_call(
        _glu_gate_kernel,
        grid=(b, s // tm, nt),
        in_specs=[
            pl.BlockSpec((None, tm, d), lambda b, i, j: (b, i, 0)),
            pl.BlockSpec((d, tn), lambda b, i, j: (0, j)),
            pl.BlockSpec((d, tn), lambda b, i, j: (0, nt + j)),
            pl.BlockSpec((None, tm, tn), lambda b, i, j: (b, i, j)),
            pl.BlockSpec((None, 1, tn), lambda b, i, j: (b, 0, gate_chunk * nt + j)),
        ],
        out_specs=pl.BlockSpec((None, tm, tn), lambda b, i, j: (b, i, j)),
        out_shape=jax.ShapeDtypeStruct((b, s, d), F32),
        compiler_params=_cparams(("parallel", "parallel", "parallel")),
        name="glu_gate",
    )(z, w_glu, w_glu, x, mod3)


def _mlp_kernel(x_ref, g_ref, sh_ref, sc_ref, w1_ref, w2_ref, gate_ref, o_ref, h_ref, acc_ref):
    j = pl.program_id(2)

    @pl.when(j == 0)
    def _():
        h_ref[...] = _rms_mod(x_ref, g_ref, sh_ref, sc_ref).astype(BF16)
        acc_ref[...] = jnp.zeros_like(acc_ref)

    a = jnp.maximum(jnp.dot(h_ref[...], w1_ref[...], preferred_element_type=F32), 0.0)
    acc_ref[...] += jnp.dot((a * a).astype(BF16), w2_ref[...], preferred_element_type=F32)

    @pl.when(j == pl.num_programs(2) - 1)
    def _():
        o_ref[...] = x_ref[...] + gate_ref[...] * acc_ref[...]


def _mlp(x, ln_g, mod3, w1, w2, tm=512, tf=1024):
    b, s, d = x.shape
    f = w1.shape[1]
    return pl.pallas_call(
        _mlp_kernel,
        grid=(b, s // tm, f // tf),
        in_specs=[
            pl.BlockSpec((None, tm, d), lambda b, i, j: (b, i, 0)),
            pl.BlockSpec((1, d), lambda b, i, j: (0, 0)),
            *_mod_specs(d, 3, 4),
            pl.BlockSpec((d, tf), lambda b, i, j: (0, j)),
            pl.BlockSpec((tf, d), lambda b, i, j: (j, 0)),
            pl.BlockSpec((None, 1, d), lambda b, i, j: (b, 0, 5)),
        ],
        out_specs=pl.BlockSpec((None, tm, d), lambda b, i, j: (b, i, 0)),
        out_shape=jax.ShapeDtypeStruct((b, s, d), F32),
        scratch_shapes=[pltpu.VMEM((tm, d), BF16), pltpu.VMEM((tm, d), F32)],
        compiler_params=_cparams(("parallel", "parallel", "arbitrary")),
        name="mlp",
    )(x, ln_g.reshape(1, d), mod3, mod3, w1, w2, mod3)


def _sb_attn_kernel(q_ref, k_ref, v_ref, o_ref, *, t):
    i = pl.program_id(2)
    q = q_ref[...]
    row = lax.broadcasted_iota(jnp.int32, (t, t), 0)
    col = lax.broadcasted_iota(jnp.int32, (t, t), 1)
    later_mat = (row > col).astype(BF16)
    causal = col < row

    def block(start, carry, acc, diag):
        kb = k_ref[pl.ds(start, t), :]
        vb = v_ref[pl.ds(start, t), :]
        z = lax.dot_general(q, kb, (((1,), (1,)), ((), ())), preferred_element_type=F32)
        log_not = jnp.minimum(-z, 0.0) - jnp.log(1.0 + jnp.exp(-jnp.abs(z)))
        if diag:
            log_not = jnp.where(causal, log_not, 0.0)
        hi = log_not.astype(BF16)
        lo = (log_not - hi.astype(F32)).astype(BF16)
        later = (jnp.dot(hi, later_mat, preferred_element_type=F32)
                 + jnp.dot(lo, later_mat, preferred_element_type=F32))
        w = jnp.exp(z + log_not + later + carry)
        if diag:
            w = jnp.where(causal, w, 0.0)
        acc = acc + jnp.dot(w.astype(BF16), vb, preferred_element_type=F32)
        carry = carry + later[:, 0:1] + log_not[:, 0:1]
        return carry, acc

    carry0 = jnp.zeros((t, 1), F32)
    acc0 = jnp.zeros((t, HEAD_DIM), F32)
    carry, acc = block(pl.multiple_of(i * t, t), carry0, acc0, True)

    def body(n, c):
        start = pl.multiple_of((i - 1 - n) * t, t)
        return block(start, c[0], c[1], False)

    carry, acc = lax.fori_loop(0, i, body, (carry, acc))
    o_ref[...] = acc.astype(o_ref.dtype)


def _sb_attention(qkv, n_heads, t=256):
    b, s, _ = qkv.shape
    return pl.pallas_call(
        functools.partial(_sb_attn_kernel, t=t),
        grid=(b, n_heads, s // t),
        in_specs=[
            pl.BlockSpec((None, t, HEAD_DIM), lambda b, h, i: (b, i, h)),
            pl.BlockSpec((None, s, HEAD_DIM), lambda b, h, i: (b, 0, n_heads + h)),
            pl.BlockSpec((None, s, HEAD_DIM), lambda b, h, i: (b, 0, 2 * n_heads + h)),
        ],
        out_specs=pl.BlockSpec((None, t, HEAD_DIM), lambda b, h, i: (b, i, h)),
        out_shape=jax.ShapeDtypeStruct((b, s, n_heads * HEAD_DIM), BF16),
        compiler_params=_cparams(("parallel", "parallel", "parallel")),
        name="sb_attn",
    )(qkv, qkv, qkv)


def _s5_tables(lam_re, lam_im, log_dt, b_re, b_im, c_re, c_im):
    g, p, gc = b_re.shape
    l = S5_CHUNK
    gpt = LANES // gc
    nt = g // gpt
    hi = lax.Precision.HIGHEST
    dt = jnp.exp(log_dt.astype(F32))[:, None]
    lr = lam_re.astype(F32)
    li = lam_im.astype(F32)
    mag = jnp.exp(lr * dt)
    ar = mag * jnp.cos(li * dt)
    ai = mag * jnp.sin(li * dt)
    den = lr * lr + li * li
    fr = ((ar - 1.0) * lr + ai * li) / den
    fi = (ai * lr - (ar - 1.0) * li) / den
    br_ = b_re.astype(F32)
    bi_ = b_im.astype(F32)
    bbr = fr[..., None] * br_ - fi[..., None] * bi_
    bbi = fr[..., None] * bi_ + fi[..., None] * br_
    cr = c_re.astype(F32)
    ci = c_im.astype(F32)
    n = jnp.arange(l + 1, dtype=F32)[:, None, None]
    pw_r = jnp.exp(n * (lr * dt)) * jnp.cos(n * (li * dt))
    pw_i = jnp.exp(n * (lr * dt)) * jnp.sin(n * (li * dt))
    abr = pw_r[..., None] * bbr - pw_i[..., None] * bbi
    abi = pw_r[..., None] * bbi + pw_i[..., None] * bbr
    eye = jnp.eye(gpt, dtype=F32)

    wz = jnp.stack([abr[:l][::-1], abi[:l][::-1]], axis=0)
    wz = wz.transpose(2, 1, 4, 0, 3).reshape(nt, gpt, l, gc, 2, p)
    wz = jnp.einsum('tgjdrp,gh->tjgdrhp', wz, eye).reshape(nt, l * LANES, 2 * gpt * p)

    kern = (jnp.einsum('gcp,ngpd->ngcd', cr, abr[:l], precision=hi)
            - jnp.einsum('gcp,ngpd->ngcd', ci, abi[:l], precision=hi))
    tau = jnp.arange(l)[None, :] - jnp.arange(l)[:, None]
    kji = kern[jnp.clip(tau, 0, l - 1)] * (tau >= 0)[:, :, None, None, None].astype(F32)
    intra = kji.transpose(2, 0, 4, 1, 3).reshape(nt, gpt, l, gc, l, gc)
    intra = jnp.einsum('tgjdic,gh->tjgdihc', intra, eye).reshape(nt, l * LANES, l * LANES)

    pr1 = pw_r[1:].transpose(1, 2, 0)[..., None]
    pi1 = pw_i[1:].transpose(1, 2, 0)[..., None]
    crt = cr.transpose(0, 2, 1)[:, :, None, :]
    cit = ci.transpose(0, 2, 1)[:, :, None, :]
    vr = (crt * pr1 - cit * pi1).reshape(nt, gpt, p, l, gc)
    vi = (-(crt * pi1 + cit * pr1)).reshape(nt, gpt, p, l, gc)
    vr = jnp.einsum('tgpic,gh->tgpihc', vr, eye).reshape(nt, gpt * p, l * LANES)
    vi = jnp.einsum('tgpic,gh->tgpihc', vi, eye).reshape(nt, gpt * p, l * LANES)
    wy = jnp.concatenate([intra, vr, vi], axis=1)

    alr = pw_r[l].reshape(nt, 1, gpt * p)
    ali = pw_i[l].reshape(nt, 1, gpt * p)
    return wz.astype(BF16), wy.astype(BF16), alr, ali


def _s5_core_kernel(ut_ref, wz_ref, wy_ref, alr_ref, ali_ref, d_ref, o_ref,
                    zr_ref, zi_ref, hr_ref, hi_ref, *, l, nk):
    us = [ut_ref[j] for j in range(l)]
    ub = jnp.concatenate([u.astype(BF16) for u in us], axis=1)
    z = jnp.dot(ub, wz_ref[...], preferred_element_type=F32)
    half = z.shape[1] // 2
    zr_ref[...] = z[:, :half]
    zi_ref[...] = z[:, half:]
    alr = alr_ref[...]
    ali = ali_ref[...]
    rows = lax.broadcasted_iota(jnp.int32, (SUBLANES, half), 0)

    def body(kb, c):
        hr, hi = c
        base = pl.multiple_of(kb * SUBLANES, SUBLANES)
        zr = zr_ref[pl.ds(base, SUBLANES), :]
        zi = zi_ref[pl.ds(base, SUBLANES), :]
        out_r = jnp.zeros((SUBLANES, half), F32)
        out_i = jnp.zeros((SUBLANES, half), F32)
        for r in range(SUBLANES):
            out_r = jnp.where(rows == r, hr, out_r)
            out_i = jnp.where(rows == r, hi, out_i)
            hr, hi = (alr * hr - ali * hi + zr[r:r + 1, :], alr * hi + ali * hr + zi[r:r + 1, :])
        hr_ref[pl.ds(base, SUBLANES), :] = out_r
        hi_ref[pl.ds(base, SUBLANES), :] = out_i
        return hr, hi

    zero = jnp.zeros((1, half), F32)
    lax.fori_loop(0, nk // SUBLANES, body, (zero, zero))

    lhs = jnp.concatenate([ub, hr_ref[...].astype(BF16), hi_ref[...].astype(BF16)], axis=1)
    y = jnp.dot(lhs, wy_ref[...], preferred_element_type=F32)
    d = d_ref[...]
    for i in range(l):
        yi = y[:, i * LANES:(i + 1) * LANES] + d * us[i]
        o_ref[i] = _gelu_tanh(yi).astype(o_ref.dtype)


def _s5_core(u, tables, d_skip):
    b, s, d = u.shape
    l = S5_CHUNK
    nk = s // l
    nt = d // LANES
    wz, wy, alr, ali = tables
    half = wz.shape[2] // 2
    ut = u.reshape(b, nk, l, nt, LANES).transpose(0, 3, 2, 1, 4)
    zt = pl.pallas_call(
        functools.partial(_s5_core_kernel, l=l, nk=nk),
        grid=(nt, b),
        in_specs=[
            pl.BlockSpec((None, None, l, nk, LANES), lambda t, b: (b, t, 0, 0, 0)),
            pl.BlockSpec((None,) + wz.shape[1:], lambda t, b: (t, 0, 0)),
            pl.BlockSpec((None,) + wy.shape[1:], lambda t, b: (t, 0, 0)),
            pl.BlockSpec((None, 1, half), lambda t, b: (t, 0, 0)),
            pl.BlockSpec((None, 1, half), lambda t, b: (t, 0, 0)),
            pl.BlockSpec((None, 1, LANES), lambda t, b: (t, 0, 0)),
        ],
        out_specs=pl.BlockSpec((None, None, l, nk, LANES), lambda t, b: (b, t, 0, 0, 0)),
        out_shape=jax.ShapeDtypeStruct((b, nt, l, nk, LANES), BF16),
        scratch_shapes=[pltpu.VMEM((nk, half), F32)] * 4,
        compiler_params=_cparams(("parallel", "parallel")),
        name="s5_core",
    )(ut, wz, wy, alr, ali, d_skip.astype(F32).reshape(nt, 1, LANES))
    return zt.transpose(0, 3, 2, 1, 4).reshape(b, s, d)


def _rope_tab_kernel(pos_ref, f_head_ref, f_idx_ref, ch_ref, sh_ref, ci_ref, si_ref):
    p = pos_ref[...].astype(F32)
    lane = lax.broadcasted_iota(jnp.int32, ch_ref.shape, 1)
    a = p * f_head_ref[...]
    ch_ref[...] = jnp.cos(a)
    sh_ref[...] = jnp.where(lane < HEAD_DIM // 2, -jnp.sin(a), jnp.sin(a))
    a = p * f_idx_ref[...]
    ci_ref[...] = jnp.cos(a)
    si_ref[...] = jnp.where(lane % IDX_DIM < IDX_DIM // 2, -jnp.sin(a), jnp.sin(a))


def _rope_tables(positions, tm=1024):
    b, s = positions.shape

    def inv_freq(dim):
        return ROPE_THETA ** (-jnp.arange(0, dim, 2, dtype=F32) / dim)

    f_head = jnp.tile(inv_freq(HEAD_DIM), 2).reshape(1, LANES)
    f_idx = jnp.tile(inv_freq(IDX_DIM), 2 * LANES // IDX_DIM).reshape(1, LANES)
    row_spec = pl.BlockSpec((None, tm, LANES), lambda b, i: (b, i, 0))
    return pl.pallas_call(
        _rope_tab_kernel,
        grid=(b, s // tm),
        in_specs=[
            pl.BlockSpec((None, tm, 1), lambda b, i: (b, i, 0)),
            pl.BlockSpec((1, LANES), lambda b, i: (0, 0)),
            pl.BlockSpec((1, LANES), lambda b, i: (0, 0)),
        ],
        out_specs=[row_spec] * 4,
        out_shape=[jax.ShapeDtypeStruct((b, s, LANES), F32)] * 4,
        compiler_params=_cparams(("parallel", "parallel")),
        name="rope_tables",
    )(positions.reshape(b, s, 1), f_head, f_idx)


def _rope_head(y, cos, sin_signed):
    return y * cos + pltpu.roll(y, HEAD_DIM // 2, axis=1) * sin_signed


def _rope_idx(y, cos, sin_signed):
    lane = lax.broadcasted_iota(jnp.int32, y.shape, 1)
    half = IDX_DIM // 2
    partner = jnp.where(lane % IDX_DIM < half, pltpu.roll(y, LANES - half, axis=1), pltpu.roll(y, half, axis=1))
    return y * cos + partner * sin_signed


def _dsa_inproj_kernel(x_ref, g_ref, sh_ref, sc_ref, w_ref, gain_ref, ch_ref, shd_ref, ci_ref, si_ref,
                       o_ref, tail_ref, h_ref, *, n_qk, n_v, n_qi):
    j = pl.program_id(2)

    @pl.when(j == 0)
    def _():
        h_ref[...] = _rms_mod(x_ref, g_ref, sh_ref, sc_ref).astype(BF16)

    y = jnp.dot(h_ref[...], w_ref[...], preferred_element_type=F32)
    slices = [slice(hh * LANES, (hh + 1) * LANES) for hh in range(y.shape[1] // LANES)]

    @pl.when(j < n_qk)
    def _():
        for sl in slices:
            yn = _head_rmsnorm(y[:, sl], gain_ref[:, sl])
            o_ref[:, sl] = _rope_head(yn, ch_ref[...], shd_ref[...]).astype(o_ref.dtype)

    @pl.when((j >= n_qk) & (j < n_qk + n_v))
    def _():
        o_ref[...] = y.astype(o_ref.dtype)

    @pl.when((j >= n_qk + n_v) & (j < n_qk + n_v + n_qi))
    def _():
        for sl in slices:
            o_ref[:, sl] = (_rope_idx(y[:, sl], ci_ref[...], si_ref[...]) * IDX_DIM ** -0.5).astype(o_ref.dtype)

    @pl.when(j == n_qk + n_v + n_qi)
    def _():
        t = y[:, :LANES]
        lane = lax.broadcasted_iota(jnp.int32, t.shape, 1)
        tail_ref[...] = jnp.where(lane < IDX_DIM, _rope_idx(t, ci_ref[...], si_ref[...]), t * IDX_HEADS ** -0.5)


def _dsa_inproj(x, ln_g, mod3, w_pad, gain_row, tabs, n_heads, tm=1024, tn=512):
    b, s, d = x.shape
    dm = n_heads * HEAD_DIM
    n_qk, n_v, n_qi = 2 * dm // tn, dm // tn, IDX_HEADS * IDX_DIM // tn
    n_main = n_qk + n_v + n_qi
    tab_spec = pl.BlockSpec((None, tm, LANES), lambda b, i, j: (b, i, 0))
    return pl.pallas_call(
        functools.partial(_dsa_inproj_kernel, n_qk=n_qk, n_v=n_v, n_qi=n_qi),
        grid=(b, s // tm, n_main + 1),
        in_specs=[
            pl.BlockSpec((None, tm, d), lambda b, i, j: (b, i, 0)),
            pl.BlockSpec((1, d), lambda b, i, j: (0, 0)),
            *_mod_specs(d, 0, 1),
            pl.BlockSpec((d, tn), lambda b, i, j: (0, j)),
            pl.BlockSpec((1, tn), lambda b, i, j: (0, j)),
            tab_spec, tab_spec, tab_spec, tab_spec,
        ],
        out_specs=[
            pl.BlockSpec((None, tm, tn), lambda b, i, j: (b, i, jnp.minimum(j, n_main - 1))),
            pl.BlockSpec((None, tm, LANES), lambda b, i, j: (b, i, 0)),
        ],
        out_shape=[jax.ShapeDtypeStruct((b, s, n_main * tn), BF16),
                   jax.ShapeDtypeStruct((b, s, LANES), F32)],
        scratch_shapes=[pltpu.VMEM((tm, d), BF16)],
        compiler_params=_cparams(("parallel", "parallel", "arbitrary")),
        name="dsa_inproj",
    )(x, ln_g.reshape(1, d), mod3, mod3, w_pad, gain_row, *tabs)


def _dsa_core_kernel(q_ref, k_ref, v_ref, qi_ref, ki_ref, wi_ref, o_ref, key_ref, bias_ref, *, t, topk, idx_bits):
    i = pl.program_id(1)
    h = pl.program_id(2)
    nk = i + 1
    kf = float(topk)

    def chunk(c):
        return pl.ds(pl.multiple_of(c * t, t), t)

    @pl.when(h == 0)
    def _():
        qi = qi_ref[...]
        wi = wi_ref[...]
        row = lax.broadcasted_iota(jnp.int32, (t, t), 0)
        col = lax.broadcasted_iota(jnp.int32, (t, t), 1)

        def score_body(c, _):
            ki = ki_ref[chunk(c), :][:, :IDX_DIM].astype(BF16)
            score = jnp.zeros((t, t), F32)
            for hh in range(IDX_HEADS):
                rel = lax.dot_general(qi[:, hh * IDX_DIM:(hh + 1) * IDX_DIM], ki, (((1,), (1,)), ((), ())),
                                      preferred_element_type=F32)
                score = score + wi[:, IDX_DIM + hh:IDX_DIM + hh + 1] * jnp.maximum(rel, 0.0)
            bits = pltpu.bitcast(score + 0.0, jnp.int32)
            key = jnp.where(bits < 0, bits ^ jnp.int32(0x7FFFFFFF), bits)
            key = jnp.where((c - i) * t + col <= row, key, jnp.int32(INT_MIN))
            key_ref[:, chunk(c)] = key
            return 0

        lax.fori_loop(0, nk, score_body, 0)

        def count(pred):
            def body(c, acc):
                m = jnp.where(pred(key_ref[:, chunk(c)], c * t + col), 1.0, 0.0)
                part = m[:, 0:LANES]
                for s in range(1, t // LANES):
                    part = part + m[:, s * LANES:(s + 1) * LANES]
                return acc + part
            acc = lax.fori_loop(0, nk, body, jnp.zeros((t, LANES), F32))
            return jnp.sum(acc, axis=1, keepdims=True)

        def bisect(n, thr):
            cand = thr + lax.shift_left(jnp.int32(1), 31 - n)
            return jnp.where(count(lambda key, pos: key >= cand) >= kf, cand, thr)

        thr = lax.fori_loop(0, 32, bisect, jnp.full((t, 1), INT_MIN, jnp.int32))
        thr = jnp.maximum(thr, jnp.int32(INT_MIN + 1))
        n_ge = count(lambda key, pos: key >= thr)

        def tie_cut(_):
            need = kf - count(lambda key, pos: key > thr)
            def step(n, cut):
                cand = cut + lax.shift_left(jnp.int32(1), idx_bits - 1 - n)
                below = count(lambda key, pos: (key == thr) & (pos < cand))
                return jnp.where(below < need, cand, cut)
            return lax.fori_loop(0, idx_bits, step, jnp.zeros((t, 1), jnp.int32))

        cut = lax.cond(jnp.max(n_ge) > kf, tie_cut, lambda _: jnp.full((t, 1), 2 ** idx_bits - 1, jnp.int32), 0)

        def bias_body(c, _):
            key = key_ref[:, chunk(c)]
            sel = (key > thr) | ((key == thr) & (c * t + col <= cut))
            bias_ref[:, chunk(c)] = jnp.where(sel, 0.0, NEG_BIG)
            return 0

        lax.fori_loop(0, nk, bias_body, 0)

    q = q_ref[...]

    def attn_body(c, carry):
        m, l, acc = carry
        s = lax.dot_general(q, k_ref[chunk(c), :], (((1,), (1,)), ((), ())), preferred_element_type=F32)
        s = s + bias_ref[:, chunk(c)]
        m_new = jnp.maximum(m, jnp.max(s, axis=1, keepdims=True))
        alpha = jnp.exp(m - m_new)
        p = jnp.exp(s - m_new)
        l = alpha * l + jnp.sum(p, axis=1, keepdims=True)
        acc = alpha * acc + jnp.dot(p.astype(BF16), v_ref[chunk(c), :], preferred_element_type=F32)
        return m_new, l, acc

    m0 = jnp.full((t, 1), NEG_BIG, F32)
    l0 = jnp.zeros((t, 1), F32)
    acc0 = jnp.zeros((t, HEAD_DIM), F32)
    _, l, acc = lax.fori_loop(0, nk, attn_body, (m0, l0, acc0))
    o_ref[...] = (acc / l).astype(o_ref.dtype)


def _dsa_core(main, tail, n_heads, t=512):
    b, s, _ = main.shape
    topk = min(DSA_TOPK_MAX, s // 4)
    idx_bits = max(1, (s - 1).bit_length())
    qi_block = (3 * n_heads * HEAD_DIM) // (IDX_HEADS * IDX_DIM)
    return pl.pallas_call(
        functools.partial(_dsa_core_kernel, t=t, topk=topk, idx_bits=idx_bits),
        grid=(b, s // t, n_heads),
        in_specs=[
            pl.BlockSpec((None, t, HEAD_DIM), lambda b, i, h: (b, i, h)),
            pl.BlockSpec((None, s, HEAD_DIM), lambda b, i, h: (b, 0, n_heads + h)),
            pl.BlockSpec((None, s, HEAD_DIM), lambda b, i, h: (b, 0, 2 * n_heads + h)),
            pl.BlockSpec((None, t, IDX_HEADS * IDX_DIM), lambda b, i, h: (b, i, qi_block)),
            pl.BlockSpec((None, s, LANES), lambda b, i, h: (b, 0, 0)),
            pl.BlockSpec((None, t, LANES), lambda b, i, h: (b, i, 0)),
        ],
        out_specs=pl.BlockSpec((None, t, HEAD_DIM), lambda b, i, h: (b, i, h)),
        out_shape=jax.ShapeDtypeStruct((b, s, n_heads * HEAD_DIM), BF16),
        scratch_shapes=[pltpu.VMEM((t, s), jnp.int32), pltpu.VMEM((t, s), F32)],
        compiler_params=_cparams(("parallel", "parallel", "arbitrary")),
        name="dsa_core",
    )(main, main, main, main, tail, tail)


def kernel(x, c, positions, ln1_g, ln2_g, ada_w, ada_b, mlp_w1, mlp_w2, sb_w_in, sb_q_gain, sb_k_gain, sb_w_out, s5_w_in, s5_lambda_re, s5_lambda_im, s5_log_dt, s5_b_re, s5_b_im, s5_c_re, s5_c_im, s5_d, s5_w_glu, dsa_w_in, dsa_q_gain, dsa_k_gain, dsa_w_out):
    depth = ada_w.shape[0]
    b, s, d = x.shape
    n_heads = d // HEAD_DIM
    scale = HEAD_DIM ** -0.5

    mod = _ada_mod(c, ada_w, ada_b)
    rope_tabs = _rope_tables(positions) if depth > 2 else None

    counts = [0, 0, 0]
    for i in range(depth):
        mod3 = mod[i].reshape(SUBLANES, 1, 6 * d)
        kind = i % N_MIXERS
        j = counts[kind]
        counts[kind] += 1
        if kind == 0:
            gain = jnp.concatenate([jnp.tile(sb_q_gain[j].astype(F32) * scale, n_heads),
                                    jnp.tile(sb_k_gain[j].astype(F32), n_heads),
                                    jnp.ones((d,), F32)]).reshape(1, 3 * d)
            qkv = _sb_inproj(x, ln1_g[i], mod3, sb_w_in[j].astype(BF16), gain)
            o = _sb_attention(qkv, n_heads)
            x = _out_gate(o, sb_w_out[j].astype(BF16), x, mod3, 2)
        elif kind == 1:
            u = _nm_matmul(x, ln1_g[i], mod3, s5_w_in[j].astype(BF16), F32)
            tables = _s5_tables(s5_lambda_re[j], s5_lambda_im[j], s5_log_dt[j], s5_b_re[j], s5_b_im[j],
                                s5_c_re[j], s5_c_im[j])
            z = _s5_core(u, tables, s5_d[j])
            x = _glu_gate(z, s5_w_glu[j].astype(BF16), x, mod3, 2)
        else:
            tn = 512
            n_in = dsa_w_in.shape[2]
            n_main = 3 * d + IDX_HEADS * IDX_DIM
            w_pad = jnp.zeros((d, n_main + tn), BF16).at[:, :n_in].set(dsa_w_in[j].astype(BF16))
            gain = jnp.zeros((1, n_main + tn), F32)
            gain = gain.at[0, :d].set(jnp.tile(dsa_q_gain[j].astype(F32) * scale, n_heads))
            gain = gain.at[0, d:2 * d].set(jnp.tile(dsa_k_gain[j].astype(F32), n_heads))
            main, tail = _dsa_inproj(x, ln1_g[i], mod3, w_pad, gain, rope_tabs, n_heads, tn=tn)
            o = _dsa_core(main, tail, n_heads)
            x = _out_gate(o, dsa_w_out[j].astype(BF16), x, mod3, 2)
        x = _mlp(x, ln2_g[i], mod3, mlp_w1[i].astype(BF16), mlp_w2[i].astype(BF16))
    return x
```

```python
import functools
import math

import jax
import jax.numpy as jnp
from jax import lax
from jax.experimental import pallas as pl
from jax.experimental.pallas import tpu as pltpu

F32 = jnp.float32
BF16 = jnp.bfloat16

N_MIXERS = 3
HEAD_DIM = 128
ROPE_THETA = 10000.0
EPS = 1e-6
S5_GROUP = 16
S5_STATE = 64
IDX_HEADS = 16
IDX_DIM = 64
DSA_TOPK_MAX = 256

LANES = 128
SUBLANES = 8
VMEM_LIMIT_BYTES = 56 * 1024 * 1024

S5_CHUNK = 16
NEG_BIG = -1e30
INT_MIN = -(2 ** 31)
LOG2_E = 1.4426950408889634


def _cparams(sem):
    return pltpu.CompilerParams(dimension_semantics=sem, vmem_limit_bytes=VMEM_LIMIT_BYTES)


def _sigmoid(x):
    return 1.0 / (1.0 + jnp.exp(-x))


def _gelu_tanh(x):
    c = math.sqrt(2.0 / math.pi)
    return 0.5 * x * (1.0 + jnp.tanh(c * (x + 0.044715 * (x * x * x))))


def _rms_mod(x_ref, g_ref, sh_ref, sc_ref):
    x = x_ref[...]
    ms = jnp.mean(x * x, axis=-1, keepdims=True)
    y = x * lax.rsqrt(ms + EPS) * g_ref[...]
    return y * (1.0 + sc_ref[...]) + sh_ref[...]


def _head_rmsnorm(y, gain):
    ms = jnp.mean(y * y, axis=-1, keepdims=True)
    return y * lax.rsqrt(ms + EPS) * gain


def _ada_kernel(c_ref, w_ref, b_ref, o_ref):
    c = c_ref[...]
    cond = (c * _sigmoid(c)).astype(BF16)
    o_ref[...] = jnp.dot(cond, w_ref[...].astype(BF16), preferred_element_type=F32) + b_ref[...]


def _ada_mod(c, ada_w, ada_b):
    depth, d, n = ada_w.shape
    b = c.shape[0]
    c8 = jnp.zeros((SUBLANES, d), F32).at[:b].set(c)
    tn = 1024
    return pl.pallas_call(
        _ada_kernel,
        grid=(depth, n // tn),
        in_specs=[
            pl.BlockSpec((SUBLANES, d), lambda l, j: (0, 0)),
            pl.BlockSpec((None, d, tn), lambda l, j: (l, 0, j)),
            pl.BlockSpec((None, 1, tn), lambda l, j: (l, 0, j)),
        ],
        out_specs=pl.BlockSpec((None, SUBLANES, tn), lambda l, j: (l, 0, j)),
        out_shape=jax.ShapeDtypeStruct((depth, SUBLANES, n), F32),
        compiler_params=_cparams(("parallel", "parallel")),
        name="ada_mod",
    )(c8, ada_w, ada_b.reshape(depth, 1, n))


def _mod_specs(d, shift_chunk, scale_chunk):
    return [
        pl.BlockSpec((None, 1, d), lambda b, i, j: (b, 0, shift_chunk)),
        pl.BlockSpec((None, 1, d), lambda b, i, j: (b, 0, scale_chunk)),
    ]


def _nm_matmul_kernel(x_ref, g_ref, sh_ref, sc_ref, w_ref, o_ref, h_ref):
    @pl.when(pl.program_id(2) == 0)
    def _():
        h_ref[...] = _rms_mod(x_ref, g_ref, sh_ref, sc_ref).astype(BF16)

    o_ref[...] = jnp.dot(h_ref[...], w_ref[...], preferred_element_type=F32).astype(o_ref.dtype)


def _nm_matmul(x, ln_g, mod3, w, out_dtype, tm=1024, tn=1024):
    b, s, d = x.shape
    n = w.shape[1]
    return pl.pallas_call(
        _nm_matmul_kernel,
        grid=(b, s // tm, n // tn),
        in_specs=[
            pl.BlockSpec((None, tm, d), lambda b, i, j: (b, i, 0)),
            pl.BlockSpec((1, d), lambda b, i, j: (0, 0)),
            *_mod_specs(d, 0, 1),
            pl.BlockSpec((d, tn), lambda b, i, j: (0, j)),
        ],
        out_specs=pl.BlockSpec((None, tm, tn), lambda b, i, j: (b, i, j)),
        out_shape=jax.ShapeDtypeStruct((b, s, n), out_dtype),
        scratch_shapes=[pltpu.VMEM((tm, d), BF16)],
        compiler_params=_cparams(("parallel", "parallel", "arbitrary")),
        name="nm_matmul",
    )(x, ln_g.reshape(1, d), mod3, mod3, w)


def _sb_inproj_kernel(x_ref, g_ref, sh_ref, sc_ref, w_ref, gain_ref, o_ref, h_ref, *, n_norm_tiles):
    j = pl.program_id(2)

    @pl.when(j == 0)
    def _():
        h_ref[...] = _rms_mod(x_ref, g_ref, sh_ref, sc_ref).astype(BF16)

    y = jnp.dot(h_ref[...], w_ref[...], preferred_element_type=F32)

    @pl.when(j < n_norm_tiles)
    def _():
        for hh in range(y.shape[1] // HEAD_DIM):
            sl = slice(hh * HEAD_DIM, (hh + 1) * HEAD_DIM)
            o_ref[:, sl] = _head_rmsnorm(y[:, sl], gain_ref[:, sl]).astype(o_ref.dtype)

    @pl.when(j >= n_norm_tiles)
    def _():
        o_ref[...] = y.astype(o_ref.dtype)


def _sb_inproj(x, ln_g, mod3, w, gain_row, tm=1024, tn=1024):
    b, s, d = x.shape
    n = w.shape[1]
    return pl.pallas_call(
        functools.partial(_sb_inproj_kernel, n_norm_tiles=2 * d // tn),
        grid=(b, s // tm, n // tn),
        in_specs=[
            pl.BlockSpec((None, tm, d), lambda b, i, j: (b, i, 0)),
            pl.BlockSpec((1, d), lambda b, i, j: (0, 0)),
            *_mod_specs(d, 0, 1),
            pl.BlockSpec((d, tn), lambda b, i, j: (0, j)),
            pl.BlockSpec((1, tn), lambda b, i, j: (0, j)),
        ],
        out_specs=pl.BlockSpec((None, tm, tn), lambda b, i, j: (b, i, j)),
        out_shape=jax.ShapeDtypeStruct((b, s, n), BF16),
        scratch_shapes=[pltpu.VMEM((tm, d), BF16)],
        compiler_params=_cparams(("parallel", "parallel", "arbitrary")),
        name="sb_inproj",
    )(x, ln_g.reshape(1, d), mod3, mod3, w, gain_row)


def _out_gate_kernel(a_ref, w_ref, x_ref, gate_ref, o_ref):
    y = jnp.dot(a_ref[...], w_ref[...], preferred_element_type=F32)
    o_ref[...] = x_ref[...] + gate_ref[...] * y


def _out_gate(a, w, x, mod3, gate_chunk, tm=1024, tn=1024):
    b, s, d = x.shape
    k = a.shape[2]
    nt = d // tn
    return pl.pallas_call(
        _out_gate_kernel,
        grid=(b, s // tm, nt),
        in_specs=[
            pl.BlockSpec((None, tm, k), lambda b, i, j: (b, i, 0)),
            pl.BlockSpec((k, tn), lambda b, i, j: (0, j)),
            pl.BlockSpec((None, tm, tn), lambda b, i, j: (b, i, j)),
            pl.BlockSpec((None, 1, tn), lambda b, i, j: (b, 0, gate_chunk * nt + j)),
        ],
        out_specs=pl.BlockSpec((None, tm, tn), lambda b, i, j: (b, i, j)),
        out_shape=jax.ShapeDtypeStruct((b, s, d), F32),
        compiler_params=_cparams(("parallel", "parallel", "parallel")),
        name="out_gate",
    )(a, w, x, mod3)


def _glu_gate_kernel(z_ref, wa_ref, wg_ref, x_ref, gate_ref, o_ref):
    z = z_ref[...]
    a = jnp.dot(z, wa_ref[...], preferred_element_type=F32)
    g = jnp.dot(z, wg_ref[...], preferred_element_type=F32)
    o_ref[...] = x_ref[...] + gate_ref[...] * (a * _sigmoid(g))


def _glu_gate(z, w_glu, x, mod3, gate_chunk, tm=1024, tn=512):
    b, s, d = x.shape
    nt = d // tn
    return pl.pallas_call(
        _glu_gate_kernel,
        grid=(b, s // tm, nt),
        in_specs=[
            pl.BlockSpec((None, tm, d), lambda b, i, j: (b, i, 0)),
            pl.BlockSpec((d, tn), lambda b, i, j: (0, j)),
            pl.BlockSpec((d, tn), lambda b, i, j: (0, nt + j)),
            pl.BlockSpec((None, tm, tn), lambda b, i, j: (b, i, j)),
            pl.BlockSpec((None, 1, tn), lambda b, i, j: (b, 0, gate_chunk * nt + j)),
        ],
        out_specs=pl.BlockSpec((None, tm, tn), lambda b, i, j: (b, i, j)),
        out_shape=jax.ShapeDtypeStruct((b, s, d), F32),
        compiler_params=_cparams(("parallel", "parallel", "parallel")),
        name="glu_gate",
    )(z, w_glu, w_glu, x, mod3)


def _mlp_kernel(x_ref, g_ref, sh_ref, sc_ref, w1_ref, w2_ref, gate_ref, o_ref, h_ref, acc_ref):
    j = pl.program_id(2)

    @pl.when(j == 0)
    def _():
        h_ref[...] = _rms_mod(x_ref, g_ref, sh_ref, sc_ref).astype(BF16)
        acc_ref[...] = jnp.zeros_like(acc_ref)

    a = jnp.maximum(jnp.dot(h_ref[...], w1_ref[...], preferred_element_type=F32), 0.0)
    acc_ref[...] += jnp.dot((a * a).astype(BF16), w2_ref[...], preferred_element_type=F32)

    @pl.when(j == pl.num_programs(2) - 1)
    def _():
        o_ref[...] = x_ref[...] + gate_ref[...] * acc_ref[...]


def _mlp(x, ln_g, mod3, w1, w2, tm=512, tf=1024):
    b, s, d = x.shape
    f = w1.shape[1]
    return pl.pallas_call(
        _mlp_kernel,
        grid=(b, s // tm, f // tf),
        in_specs=[
            pl.BlockSpec((None, tm, d), lambda b, i, j: (b, i, 0)),
            pl.BlockSpec((1, d), lambda b, i, j: (0, 0)),
            *_mod_specs(d, 3, 4),
            pl.BlockSpec((d, tf), lambda b, i, j: (0, j)),
            pl.BlockSpec((tf, d), lambda b, i, j: (j, 0)),
            pl.BlockSpec((None, 1, d), lambda b, i, j: (b, 0, 5)),
        ],
        out_specs=pl.BlockSpec((None, tm, d), lambda b, i, j: (b, i, 0)),
        out_shape=jax.ShapeDtypeStruct((b, s, d), F32),
        scratch_shapes=[pltpu.VMEM((tm, d), BF16), pltpu.VMEM((tm, d), F32)],
        compiler_params=_cparams(("parallel", "parallel", "arbitrary")),
        name="mlp",
    )(x, ln_g.reshape(1, d), mod3, mod3, w1, w2, mod3)


def _sb_attn_kernel(q_ref, k_ref, v_ref, later_ref, o_ref, z_scr, e_scr, *, tq, tk):
    i = pl.program_id(2)
    m = (i + 1) * (tq // tk)
    n_diag = tq // tk
    q = q_ref[...]
    lead =lax.broadcasted_iota(jnp.int32, (tq, tk), 0) - lax.broadcasted_iota(jnp.int32, (tq, tk), 1)
    sign = jnp.uint32(0x80000000)

    def key_start(n):
        return pl.multiple_of(jnp.maximum(m - 1 - n, 0) * tk, tk)

    def score(n):
        kb = k_ref[pl.ds(key_start(n), tk), :]
        return lax.dot_general(q, kb, (((1,), (1,)), ((), ())), preferred_element_type=F32)

    def log_weights(n, z, masked):
        neg_abs = pltpu.bitcast(pltpu.bitcast(z, jnp.uint32) | sign, F32)
        ls = jnp.minimum(z, 0.0) - jnp.log(1.0 + jnp.exp2(neg_abs)) * LOG2_E
        ln = ls - z
        if masked:
            causal = key_start(n) - i * tq < lead
            ln = jnp.where(causal, ln, 0.0)
            ls = jnp.where(causal, ls, NEG_BIG)
        hi = ln.astype(BF16)
        lo = (ln - hi.astype(F32)).astype(BF16)
        later = jnp.dot(jnp.concatenate([hi, lo], axis=1), later_ref[...], preferred_element_type=F32)
        return ls + later, later[:, 0:1] + ln[:, 0:1]

    def accumulate(n, e, carry, acc):
        vb = v_ref[pl.ds(key_start(n), tk), :]
        w = jnp.exp2(e + carry).astype(BF16)
        return acc + jnp.dot(w, vb, preferred_element_type=F32)

    e0, tot0 = log_weights(0, score(0), True)
    e_scr[0] = e0
    z_scr[0] = score(1)

    def step(n, c):
        carry, tot, acc = c
        slot = n % 2
        z_scr[1 - slot] = score(n + 2)
        acc = accumulate(n, e_scr[slot], carry, acc)
        e_next, tot_next = log_weights(n + 1, z_scr[slot], True)
        e_scr[1 - slot] = e_next
        return carry + tot, tot_next, acc

    def pair(p, c):
        n = n_diag - 1 + 2 * p
        carry, tot, acc = c
        z_mid = score(n + 2)
        acc = accumulate(n, e_scr[1], carry, acc)
        e_mid, tot_mid = log_weights(n + 1, z_scr[1], False)
        carry = carry + tot
        z_scr[1] = score(n + 3)
        acc = accumulate(n + 1, e_mid, carry, acc)
        e_last, tot_last = log_weights(n + 2, z_mid, False)
        e_scr[1] = e_last
        return carry + tot_mid, tot_last, acc

    state = (jnp.zeros((tq, 1), F32), tot0, jnp.zeros((tq, HEAD_DIM), F32))
    assert n_diag % 2 == 0
    for n in range(n_diag - 1):
        state = step(n, state)
    carry, _, acc = lax.fori_loop(0, (m - n_diag) // 2, pair, state)
    acc = accumulate(m - 1, e_scr[1], carry, acc)
    o_ref[...] = acc.astype(o_ref.dtype)


def _sb_attention(qkv, n_heads, tq=512, tk=256):
    b, s, _ = qkv.shape
    pos = jnp.arange(tk)
    later_mat = jnp.tile((pos[:, None] > pos[None, :]).astype(BF16), (2, 1))
    return pl.pallas_call(
        functools.partial(_sb_attn_kernel, tq=tq, tk=tk),
        grid=(b, n_heads, s // tq),
        in_specs=[
            pl.BlockSpec((None, tq, HEAD_DIM), lambda b, h, i: (b, i, h)),
            pl.BlockSpec((None, s, HEAD_DIM), lambda b, h, i: (b, 0, n_heads + h)),
            pl.BlockSpec((None, s, HEAD_DIM), lambda b, h, i: (b, 0, 2 * n_heads + h)),
            pl.BlockSpec((2 * tk, tk), lambda b, h, i: (0, 0)),
        ],
        out_specs=pl.BlockSpec((None, tq, HEAD_DIM), lambda b, h, i: (b, i, h)),
        out_shape=jax.ShapeDtypeStruct((b, s, n_heads * HEAD_DIM), BF16),
        scratch_shapes=[pltpu.VMEM((2, tq, tk), F32), pltpu.VMEM((2, tq, tk), F32)],
        compiler_params=_cparams(("parallel", "parallel", "parallel")),
        name="sb_attn",
    )(qkv, qkv, qkv, later_mat)


def _s5_tables(lam_re, lam_im, log_dt, b_re, b_im, c_re, c_im):
    g, p, gc = b_re.shape
    l = S5_CHUNK
    gpt = LANES // gc
    nt = g // gpt
    hi = lax.Precision.HIGHEST
    dt = jnp.exp(log_dt.astype(F32))[:, None]
    lr = lam_re.astype(F32)
    li = lam_im.astype(F32)
    mag = jnp.exp(lr * dt)
    ar = mag * jnp.cos(li * dt)
    ai = mag * jnp.sin(li * dt)
    den = lr * lr + li * li
    fr = ((ar - 1.0) * lr + ai * li) / den
    fi = (ai * lr - (ar - 1.0) * li) / den
    br_ = b_re.astype(F32)
    bi_ = b_im.astype(F32)
    bbr = fr[..., None] * br_ - fi[..., None] * bi_
    bbi = fr[..., None] * bi_ + fi[..., None] * br_
    cr = c_re.astype(F32)
    ci = c_im.astype(F32)
    n = jnp.arange(l + 1, dtype=F32)[:, None, None]
    pw_r = jnp.exp(n * (lr * dt)) * jnp.cos(n * (li * dt))
    pw_i = jnp.exp(n * (lr * dt)) * jnp.sin(n * (li * dt))
    abr = pw_r[..., None] * bbr - pw_i[..., None] * bbi
    abi = pw_r[..., None] * bbi + pw_i[..., None] * bbr
    wz = jnp.stack([abr[:l][::-1], abi[:l][::-1]], axis=0)
    wz = wz.transpose(2, 1, 4, 0, 3).reshape(nt, gpt, l, gc, 2 * p)
    wz = wz.transpose(0, 2, 1, 3, 4).reshape(nt, l * LANES, 2 * p)

    kern = (jnp.einsum('gcp,ngpd->ngcd', cr, abr[:l], precision=hi)
            - jnp.einsum('gcp,ngpd->ngcd', ci, abi[:l], precision=hi))
    tau = jnp.arange(l)[None, :] - jnp.arange(l)[:, None]
    kji = kern[jnp.clip(tau, 0, l - 1)] * (tau >= 0)[:, :, None, None, None].astype(F32)
    intra = kji.transpose(2, 0, 4, 1, 3).reshape(nt, gpt, l, gc, l * gc)
    intra = intra.transpose(0, 2, 1, 3, 4).reshape(nt, l * LANES, l * gc)

    pr1 = pw_r[1:].transpose(1, 2, 0)[..., None]
    pi1 = pw_i[1:].transpose(1, 2, 0)[..., None]
    crt = cr.transpose(0, 2, 1)[:, :, None, :]
    cit = ci.transpose(0, 2, 1)[:, :, None, :]
    vr = (crt * pr1 - cit * pi1).reshape(nt, gpt * p, l * gc)
    vi = (-(crt * pi1 + cit * pr1)).reshape(nt, gpt * p, l * gc)
    wy = jnp.concatenate([intra, vr, vi], axis=1)

    alr = pw_r[l].reshape(nt, 1, gpt * p)
    ali = pw_i[l].reshape(nt, 1, gpt * p)
    return wz.astype(BF16), wy.astype(BF16), alr, ali


def _spread_groups(src_ref, dst_ref, row_group_div, col_unit):
    n_src = src_ref.shape[1]
    n_dst = dst_ref.shape[1]
    groups = n_dst // n_src
    rc = 256
    sr = lax.broadcasted_iota(jnp.int32, (n_src, n_dst), 0)
    dc = lax.broadcasted_iota(jnp.int32, (n_src, n_dst), 1)
    spread = ((sr // col_unit == dc // (col_unit * groups)) & (sr % col_unit == dc % col_unit))
    spread = jnp.where(spread, 1.0, 0.0).astype(BF16)
    rr = lax.broadcasted_iota(jnp.int32, (rc, n_dst), 0)
    cg = (lax.broadcasted_iota(jnp.int32, (rc, n_dst), 1) // col_unit) % groups
    for r0 in range(0, src_ref.shape[0], rc):
        wide = jnp.dot(src_ref[r0:r0 + rc, :], spread, preferred_element_type=F32)
        keep = ((r0 + rr) // row_group_div) % groups == cg
        dst_ref[r0:r0 + rc, :] = jnp.where(keep, wide, 0.0).astype(dst_ref.dtype)


def _s5_core_kernel(ut_ref, wzc_ref, wyc_ref, alr_ref, ali_ref, d_ref, o_ref,
                    wz_ref, wy_ref, zr_ref, zi_ref, hr_ref, hi_ref, *, l, nk, gc, p):
    @pl.when(pl.program_id(1) == 0)
    def _():
        n_in = l * LANES
        _spread_groups(wzc_ref, wz_ref, gc, p)
        _spread_groups(wyc_ref.at[:n_in], wy_ref.at[:n_in], gc, gc)
        _spread_groups(wyc_ref.at[n_in:], wy_ref.at[n_in:], p, gc)

    us = [ut_ref[j] for j in range(l)]
    ub = jnp.concatenate([u.astype(BF16) for u in us], axis=1)
    z = jnp.dot(ub, wz_ref[...], preferred_element_type=F32)
    half = z.shape[1] // 2
    zr_ref[...] = z[:, :half]
    zi_ref[...] = z[:, half:]
    alr = alr_ref[...]
    ali = ali_ref[...]
    rows = lax.broadcasted_iota(jnp.int32, (SUBLANES, half), 0)

    def body(kb, c):
        hr, hi = c
        base = pl.multiple_of(kb * SUBLANES, SUBLANES)
        zr = zr_ref[pl.ds(base, SUBLANES), :]
        zi = zi_ref[pl.ds(base, SUBLANES), :]
        out_r = jnp.zeros((SUBLANES, half), F32)
        out_i = jnp.zeros((SUBLANES, half), F32)
        for r in range(SUBLANES):
            out_r = jnp.where(rows == r, hr, out_r)
            out_i = jnp.where(rows == r, hi, out_i)
            hr, hi = (alr * hr - ali * hi + zr[r:r + 1, :], alr * hi + ali * hr + zi[r:r + 1, :])
        hr_ref[pl.ds(base, SUBLANES), :] = out_r
        hi_ref[pl.ds(base, SUBLANES), :] = out_i
        return hr, hi

    zero = jnp.zeros((1, half), F32)
    lax.fori_loop(0, nk // SUBLANES, body, (zero, zero))

    lhs = jnp.concatenate([ub, hr_ref[...].astype(BF16), hi_ref[...].astype(BF16)], axis=1)
    y = jnp.dot(lhs, wy_ref[...], preferred_element_type=F32)
    d = d_ref[...]
    for i in range(l):
        yi = y[:, i * LANES:(i + 1) * LANES] + d * us[i]
        o_ref[i] = _gelu_tanh(yi).astype(o_ref.dtype)


def _s5_core(u, tables, d_skip):
    b, s, d = u.shape
    l = S5_CHUNK
    nk = s // l
    nt = d // LANES
    wz, wy, alr, ali = tables
    half = alr.shape[2]
    gpt = LANES // S5_GROUP
    ut = u.reshape(b, nk, l, nt, LANES).transpose(0, 3, 2, 1, 4)
    zt = pl.pallas_call(
        functools.partial(_s5_core_kernel, l=l, nk=nk, gc=S5_GROUP, p=half // gpt),
        grid=(nt, b),
        in_specs=[
            pl.BlockSpec((None, None, l, nk, LANES), lambda t, b: (b, t, 0, 0, 0)),
            pl.BlockSpec((None,) + wz.shape[1:], lambda t, b: (t, 0, 0)),
            pl.BlockSpec((None,) + wy.shape[1:], lambda t, b: (t, 0, 0)),
            pl.BlockSpec((None, 1, half), lambda t, b: (t, 0, 0)),
            pl.BlockSpec((None, 1, half), lambda t, b: (t, 0, 0)),
            pl.BlockSpec((None, 1, LANES), lambda t, b: (t, 0, 0)),
        ],
        out_specs=pl.BlockSpec((None, None, l, nk, LANES), lambda t, b: (b, t, 0, 0, 0)),
        out_shape=jax.ShapeDtypeStruct((b, nt, l, nk, LANES), BF16),
        scratch_shapes=[pltpu.VMEM((l * LANES, 2 * half), BF16),
                        pltpu.VMEM((l * LANES + 2 * half, l * LANES), BF16)] + [pltpu.VMEM((nk, half), F32)] * 4,
        compiler_params=_cparams(("arbitrary", "arbitrary")),
        name="s5_core",
    )(ut, wz, wy, alr, ali, d_skip.astype(F32).reshape(nt, 1, LANES))
    return zt.transpose(0, 3, 2, 1, 4).reshape(b, s, d)


def _rope_tab_kernel(pos_ref, f_head_ref, f_idx_ref, ch_ref, sh_ref, ci_ref, si_ref):
    p = pos_ref[...].astype(F32)
    lane = lax.broadcasted_iota(jnp.int32, ch_ref.shape, 1)
    a = p * f_head_ref[...]
    ch_ref[...] = jnp.cos(a)
    sh_ref[...] = jnp.where(lane < HEAD_DIM // 2, -jnp.sin(a), jnp.sin(a))
    a = p * f_idx_ref[...]
    ci_ref[...] = jnp.cos(a)
    si_ref[...] = jnp.where(lane % IDX_DIM < IDX_DIM // 2, -jnp.sin(a), jnp.sin(a))


def _rope_tables(positions, tm=1024):
    b, s = positions.shape

    def inv_freq(dim):
        return ROPE_THETA ** (-jnp.arange(0, dim, 2, dtype=F32) / dim)

    f_head = jnp.tile(inv_freq(HEAD_DIM), 2).reshape(1, LANES)
    f_idx = jnp.tile(inv_freq(IDX_DIM), 2 * LANES // IDX_DIM).reshape(1, LANES)
    row_spec = pl.BlockSpec((None, tm, LANES), lambda b, i: (b, i, 0))
    return pl.pallas_call(
        _rope_tab_kernel,
        grid=(b, s // tm),
        in_specs=[
            pl.BlockSpec((None, tm, 1), lambda b, i: (b, i, 0)),
            pl.BlockSpec((1, LANES), lambda b, i: (0, 0)),
            pl.BlockSpec((1, LANES), lambda b, i: (0, 0)),
        ],
        out_specs=[row_spec] * 4,
        out_shape=[jax.ShapeDtypeStruct((b, s, LANES), F32)] * 4,
        compiler_params=_cparams(("parallel", "parallel")),
        name="rope_tables",
    )(positions.reshape(b, s, 1), f_head, f_idx)


def _rope_head(y, cos, sin_signed):
    return y * cos + pltpu.roll(y, HEAD_DIM // 2, axis=1) * sin_signed


def _rope_idx(y, cos, sin_signed):
    lane = lax.broadcasted_iota(jnp.int32, y.shape, 1)
    half = IDX_DIM // 2
    partner = jnp.where(lane % IDX_DIM < half, pltpu.roll(y, LANES - half, axis=1), pltpu.roll(y, half, axis=1))
    return y * cos + partner * sin_signed


def _dsa_inproj_kernel(x_ref, g_ref, sh_ref, sc_ref, w_ref, gain_ref, ch_ref, shd_ref, ci_ref, si_ref,
                       o_ref, tail_ref, h_ref, *, n_qk, n_v, n_qi):
    j = pl.program_id(2)

    @pl.when(j == 0)
    def _():
        h_ref[...] = _rms_mod(x_ref, g_ref, sh_ref, sc_ref).astype(BF16)

    y = jnp.dot(h_ref[...], w_ref[...], preferred_element_type=F32)
    slices = [slice(hh * LANES, (hh + 1) * LANES) for hh in range(y.shape[1] // LANES)]

    @pl.when(j < n_qk)
    def _():
        for sl in slices:
            yn = _head_rmsnorm(y[:, sl], gain_ref[:, sl])
            o_ref[:, sl] = _rope_head(yn, ch_ref[...], shd_ref[...]).astype(o_ref.dtype)

    @pl.when((j >= n_qk) & (j < n_qk + n_v))
    def _():
        o_ref[...] = y.astype(o_ref.dtype)

    @pl.when((j >= n_qk + n_v) & (j < n_qk + n_v + n_qi))
    def _():
        for sl in slices:
            o_ref[:, sl] = (_rope_idx(y[:, sl], ci_ref[...], si_ref[...]) * IDX_DIM ** -0.5).astype(o_ref.dtype)

    @pl.when(j == n_qk + n_v + n_qi)
    def _():
        t = y[:, :LANES]
        lane = lax.broadcasted_iota(jnp.int32, t.shape, 1)
        tail_ref[...] = jnp.where(lane < IDX_DIM, _rope_idx(t, ci_ref[...], si_ref[...]), t * IDX_HEADS ** -0.5)


def _dsa_inproj(x, ln_g, mod3, w_pad, gain_row, tabs, n_heads, tm=1024, tn=512):
    b, s, d = x.shape
    dm = n_heads * HEAD_DIM
    n_qk, n_v, n_qi = 2 * dm // tn, dm // tn, IDX_HEADS * IDX_DIM // tn
    n_main = n_qk + n_v + n_qi
    tab_spec = pl.BlockSpec((None, tm, LANES), lambda b, i, j: (b, i, 0))
    return pl.pallas_call(
        functools.partial(_dsa_inproj_kernel, n_qk=n_qk, n_v=n_v, n_qi=n_qi),
        grid=(b, s // tm, n_main + 1),
        in_specs=[
            pl.BlockSpec((None, tm, d), lambda b, i, j: (b, i, 0)),
            pl.BlockSpec((1, d), lambda b, i, j: (0, 0)),
            *_mod_specs(d, 0, 1),
            pl.BlockSpec((d, tn), lambda b, i, j: (0, j)),
            pl.BlockSpec((1, tn), lambda b, i, j: (0, j)),
            tab_spec, tab_spec, tab_spec, tab_spec,
        ],
        out_specs=[
            pl.BlockSpec((None, tm, tn), lambda b, i, j: (b, i, jnp.minimum(j, n_main - 1))),
            pl.BlockSpec((None, tm, LANES), lambda b, i, j: (b, i, 0)),
        ],
        out_shape=[jax.ShapeDtypeStruct((b, s, n_main * tn), BF16),
                   jax.ShapeDtypeStruct((b, s, LANES), F32)],
        scratch_shapes=[pltpu.VMEM((tm, d), BF16)],
        compiler_params=_cparams(("parallel", "parallel", "arbitrary")),
        name="dsa_inproj",
    )(x, ln_g.reshape(1, d), mod3, mod3, w_pad, gain_row, *tabs)


def _dsa_core_kernel(q_ref, k_ref, v_ref, qi_ref, ki_ref, wi_ref, o_ref, key_ref, bias_ref,
                     *, t, topk, idx_bits, strip):
    i = pl.program_id(1)
    h = pl.program_id(2)
    nk = i + 1
    kf = float(topk)

    def chunk(c):
        return pl.ds(pl.multiple_of(c * t, t), t)

    @pl.when(h == 0)
    def _():
        qi = qi_ref[...]
        wi = wi_ref[...]
        row = lax.broadcasted_iota(jnp.int32, (t, t), 0)
        col = lax.broadcasted_iota(jnp.int32, (t, t), 1)

        def score_body(c, _):
            ki = ki_ref[chunk(c), :][:, :IDX_DIM].astype(BF16)
            score = jnp.zeros((t, t), F32)
            for hh in range(IDX_HEADS):
                rel = lax.dot_general(qi[:, hh * IDX_DIM:(hh + 1) * IDX_DIM], ki, (((1,), (1,)), ((), ())),
                                      preferred_element_type=F32)
                score = score + wi[:, IDX_DIM + hh:IDX_DIM + hh + 1] * jnp.maximum(rel, 0.0)
            bits = pltpu.bitcast(score + 0.0, jnp.int32)
            key = jnp.where(bits < 0, bits ^ jnp.int32(0x7FFFFFFF), bits)
            key = jnp.where((c - i) * t + col <= row, key, jnp.int32(INT_MIN))
            key_ref[:, chunk(c)] = key
            return 0

        lax.fori_loop(0, nk, score_body, 0)

        lane = lax.broadcasted_iota(jnp.int32, (strip, LANES), 1)

        def count(pred, *row_args):
            counts = []
            for r in range(t // strip):
                rows = slice(r * strip, (r + 1) * strip)
                args = [jnp.broadcast_to(a[rows], (strip, LANES)) for a in row_args]

                def body(c, acc, rows=rows, args=args):
                    for s in range(t // LANES):
                        start = pl.multiple_of(c * t + s * LANES, LANES)
                        hit = pred(key_ref[rows, pl.ds(start, LANES)], start + lane, *args)
                        acc = acc + jnp.where(hit, 1.0, 0.0)
                    return acc

                counts.append(lax.fori_loop(0, nk, body, jnp.zeros((strip, LANES), F32)))
            return jnp.sum(jnp.concatenate(counts, axis=0), axis=1, keepdims=True)

        def bisect(n, thr):
            cand = thr + lax.shift_left(jnp.int32(1), 31 - n)
            return jnp.where(count(lambda key, pos, cd: key >= cd, cand) >= kf, cand, thr)

        thr = lax.fori_loop(0, 32, bisect, jnp.full((t, 1), INT_MIN, jnp.int32))
        thr = jnp.maximum(thr, jnp.int32(INT_MIN + 1))
        n_ge = count(lambda key, pos, th: key >= th, thr)

        def tie_cut(_):
            need = kf - count(lambda key, pos, th: key > th, thr)
            def step(n, cut):
                cand = cut + lax.shift_left(jnp.int32(1), idx_bits - 1 - n)
                below = count(lambda key, pos, th, cd: (key == th) & (pos < cd), thr, cand)
                return jnp.where(below < need, cand, cut)
            return lax.fori_loop(0, idx_bits, step, jnp.zeros((t, 1), jnp.int32))

        cut = lax.cond(jnp.max(n_ge) > kf, tie_cut, lambda _: jnp.full((t, 1), 2 ** idx_bits - 1, jnp.int32), 0)

        def bias_body(c, _):
            key = key_ref[:, chunk(c)]
            sel = (key > thr) | ((key == thr) & (c * t + col <= cut))
            bias_ref[:, chunk(c)] = jnp.where(sel, 0.0, NEG_BIG)
            return 0

        lax.fori_loop(0, nk, bias_body, 0)

    q = q_ref[...]

    def attn_body(c, carry):
        m, l, acc = carry
        s = lax.dot_general(q, k_ref[chunk(c), :], (((1,), (1,)), ((), ())), preferred_element_type=F32)
        s = s + bias_ref[:, chunk(c)]
        m_new = jnp.maximum(m, jnp.max(s, axis=1, keepdims=True))
        alpha = jnp.exp2(m - m_new)
        p = jnp.exp2(s - m_new)
        l = alpha * l + jnp.sum(p, axis=1, keepdims=True)
        acc = alpha * acc + jnp.dot(p.astype(BF16), v_ref[chunk(c), :], preferred_element_type=F32)
        return m_new, l, acc

    m0 = jnp.full((t, 1), NEG_BIG, F32)
    l0 = jnp.zeros((t, 1), F32)
    acc0 = jnp.zeros((t, HEAD_DIM), F32)
    _, l, acc = lax.fori_loop(0, nk, attn_body, (m0, l0, acc0))
    o_ref[...] = (acc / l).astype(o_ref.dtype)


def _dsa_core(main, tail, n_heads, t=512):
    b, s, _ = main.shape
    topk = min(DSA_TOPK_MAX, s // 4)
    idx_bits = max(1, (s - 1).bit_length())
    qi_block = (3 * n_heads * HEAD_DIM) // (IDX_HEADS * IDX_DIM)
    return pl.pallas_call(
        functools.partial(_dsa_core_kernel, t=t, topk=topk, idx_bits=idx_bits, strip=min(t, 128)),
        grid=(b, s // t, n_heads),
        in_specs=[
            pl.BlockSpec((None, t, HEAD_DIM), lambda b, i, h: (b, i, h)),
            pl.BlockSpec((None, s, HEAD_DIM), lambda b, i, h: (b, 0, n_heads + h)),
            pl.BlockSpec((None, s, HEAD_DIM), lambda b, i, h: (b, 0, 2 * n_heads + h)),
            pl.BlockSpec((None, t, IDX_HEADS * IDX_DIM), lambda b, i, h: (b, i, qi_block)),
            pl.BlockSpec((None, s, LANES), lambda b, i, h: (b, 0, 0)),
            pl.BlockSpec((None, t, LANES), lambda b, i, h: (b, i, 0)),
        ],
        out_specs=pl.BlockSpec((None, t, HEAD_DIM), lambda b, i, h: (b, i, h)),
        out_shape=jax.ShapeDtypeStruct((b, s, n_heads * HEAD_DIM), BF16),
        scratch_shapes=[pltpu.VMEM((t, s), jnp.int32), pltpu.VMEM((t, s), F32)],
        compiler_params=_cparams(("parallel", "parallel", "arbitrary")),
        name="dsa_core",
    )(main, main, main, main, tail, tail)


def kernel(x, c, positions, ln1_g, ln2_g, ada_w, ada_b, mlp_w1, mlp_w2, sb_w_in, sb_q_gain, sb_k_gain, sb_w_out, s5_w_in, s5_lambda_re, s5_lambda_im, s5_log_dt, s5_b_re, s5_b_im, s5_c_re, s5_c_im, s5_d, s5_w_glu, dsa_w_in, dsa_q_gain, dsa_k_gain, dsa_w_out):
    depth = ada_w.shape[0]
    b, s, d = x.shape
    n_heads = d // HEAD_DIM
    scale = HEAD_DIM ** -0.5

    mod = _ada_mod(c, ada_w, ada_b)
    rope_tabs = _rope_tables(positions) if depth > 2 else None

    counts = [0, 0, 0]
    for i in range(depth):
        mod3 = mod[i].reshape(SUBLANES, 1, 6 * d)
        kind = i % N_MIXERS
        j = counts[kind]
        counts[kind] += 1
        if kind == 0:
            gain = jnp.concatenate([jnp.tile(sb_q_gain[j].astype(F32) * (scale * LOG2_E), n_heads),
                                    jnp.tile(sb_k_gain[j].astype(F32), n_heads),
                                    jnp.ones((d,), F32)]).reshape(1, 3 * d)
            qkv = _sb_inproj(x, ln1_g[i], mod3, sb_w_in[j].astype(BF16), gain)
            o = _sb_attention(qkv, n_heads)
            x = _out_gate(o, sb_w_out[j].astype(BF16), x, mod3, 2)
        elif kind == 1:
            u = _nm_matmul(x, ln1_g[i], mod3, s5_w_in[j].astype(BF16), F32)
            tables = _s5_tables(s5_lambda_re[j], s5_lambda_im[j], s5_log_dt[j], s5_b_re[j], s5_b_im[j],
                                s5_c_re[j], s5_c_im[j])
            z = _s5_core(u, tables, s5_d[j])
            x = _glu_gate(z, s5_w_glu[j].astype(BF16), x, mod3, 2)
        else:
            tn = 512
            n_in = dsa_w_in.shape[2]
            n_main = 3 * d + IDX_HEADS * IDX_DIM
            w_pad = jnp.zeros((d, n_main + tn), BF16).at[:, :n_in].set(dsa_w_in[j].astype(BF16))
            gain = jnp.zeros((1, n_main + tn), F32)
            gain = gain.at[0, :d].set(jnp.tile(dsa_q_gain[j].astype(F32) * (scale * LOG2_E), n_heads))
            gain = gain.at[0, d:2 * d].set(jnp.tile(dsa_k_gain[j].astype(F32), n_heads))
            main, tail = _dsa_inproj(x, ln1_g[i], mod3, w_pad, gain, rope_tabs, n_heads, tn=tn)
            o = _dsa_core(main, tail, n_heads)
            x = _out_gate(o, dsa_w_out[j].astype(BF16), x, mod3, 2)
        x = _mlp(x, ln2_g[i], mod3, mlp_w1[i].astype(BF16), mlp_w2[i].astype(BF16))
    return x
```

```python
import functools
import math

import jax
import jax.numpy as jnp
from jax import lax
from jax.experimental import pallas as pl
from jax.experimental.pallas import tpu as pltpu

F32 = jnp.float32
BF16 = jnp.bfloat16

N_MIXERS = 3
HEAD_DIM = 128
ROPE_THETA = 10000.0
EPS = 1e-6
S5_GROUP = 16
S5_STATE = 64
IDX_HEADS = 16
IDX_DIM = 64
DSA_TOPK_MAX = 256

LANES = 128
SUBLANES = 8
VMEM_LIMIT_BYTES = 56 * 1024 * 1024

S5_CHUNK = 16
NEG_BIG = -1e30
INT_MIN = -(2 ** 31)
LOG2_E = 1.4426950408889634


def _cparams(sem):
    return pltpu.CompilerParams(dimension_semantics=sem, vmem_limit_bytes=VMEM_LIMIT_BYTES)


def _sigmoid(x):
    return 1.0 / (1.0 + jnp.exp(-x))


def _gelu_tanh(x):
    c = math.sqrt(2.0 / math.pi)
    return 0.5 * x * (1.0 + jnp.tanh(c * (x + 0.044715 * (x * x * x))))


def _rms_mod(x_ref, g_ref, sh_ref, sc_ref):
    x = x_ref[...]
    ms = jnp.mean(x * x, axis=-1, keepdims=True)
    y = x * lax.rsqrt(ms + EPS) * g_ref[...]
    return y * (1.0 + sc_ref[...]) + sh_ref[...]


def _head_rmsnorm(y, gain):
    ms = jnp.mean(y * y, axis=-1, keepdims=True)
    return y * lax.rsqrt(ms + EPS) * gain


def _ada_kernel(c_ref, w_ref, b_ref, o_ref):
    c = c_ref[...]
    cond = (c * _sigmoid(c)).astype(BF16)
    o_ref[...] = jnp.dot(cond, w_ref[...].astype(BF16), preferred_element_type=F32) + b_ref[...]


def _ada_mod(c, ada_w, ada_b):
    depth, d, n = ada_w.shape
    b = c.shape[0]
    c8 = jnp.zeros((SUBLANES, d), F32).at[:b].set(c)
    tn = 1024
    return pl.pallas_call(
        _ada_kernel,
        grid=(depth, n // tn),
        in_specs=[
            pl.BlockSpec((SUBLANES, d), lambda l, j: (0, 0)),
            pl.BlockSpec((None, d, tn), lambda l, j: (l, 0, j)),
            pl.BlockSpec((None, 1, tn), lambda l, j: (l, 0, j)),
        ],
        out_specs=pl.BlockSpec((None, SUBLANES, tn), lambda l, j: (l, 0, j)),
        out_shape=jax.ShapeDtypeStruct((depth, SUBLANES, n), F32),
        compiler_params=_cparams(("parallel", "parallel")),
        name="ada_mod",
    )(c8, ada_w, ada_b.reshape(depth, 1, n))


def _mod_specs(d, shift_chunk, scale_chunk):
    return [
        pl.BlockSpec((None, 1, d), lambda b, i, j: (b, 0, shift_chunk)),
        pl.BlockSpec((None, 1, d), lambda b, i, j: (b, 0, scale_chunk)),
    ]


def _nm_matmul_kernel(x_ref, g_ref, sh_ref, sc_ref, w_ref, o_ref, h_ref):
    @pl.when(pl.program_id(2) == 0)
    def _():
        h_ref[...] = _rms_mod(x_ref, g_ref, sh_ref, sc_ref).astype(BF16)

    o_ref[...] = jnp.dot(h_ref[...], w_ref[...], preferred_element_type=F32).astype(o_ref.dtype)


def _nm_matmul(x, ln_g, mod3, w, layer, out_dtype, tm=1024, tn=1024):
    b, s, d = x.shape
    n = w.shape[2]
    return pl.pallas_call(
        _nm_matmul_kernel,
        grid=(b, s // tm, n // tn),
        in_specs=[
            pl.BlockSpec((None, tm, d), lambda b, i, j: (b, i, 0)),
            pl.BlockSpec((1, d), lambda b, i, j: (0, 0)),
            *_mod_specs(d, 0, 1),
            pl.BlockSpec((None, d, tn), lambda b, i, j: (layer, 0, j)),
        ],
        out_specs=pl.BlockSpec((None, tm, tn), lambda b, i, j: (b, i, j)),
        out_shape=jax.ShapeDtypeStruct((b, s, n), out_dtype),
        scratch_shapes=[pltpu.VMEM((tm, d), BF16)],
        compiler_params=_cparams(("parallel", "parallel", "arbitrary")),
        name="nm_matmul",
    )(x, ln_g.reshape(1, d), mod3, mod3, w)


def _sb_inproj_kernel(x_ref, g_ref, sh_ref, sc_ref, w_ref, gain_ref, o_ref, h_ref, *, n_norm_tiles):
    j = pl.program_id(2)

    @pl.when(j == 0)
    def _():
        h_ref[...] = _rms_mod(x_ref, g_ref, sh_ref, sc_ref).astype(BF16)

    y = jnp.dot(h_ref[...], w_ref[...], preferred_element_type=F32)

    @pl.when(j < n_norm_tiles)
    def _():
        for hh in range(y.shape[1] // HEAD_DIM):
            sl = slice(hh * HEAD_DIM, (hh + 1) * HEAD_DIM)
            o_ref[:, sl] = _head_rmsnorm(y[:, sl], gain_ref[:, sl]).astype(o_ref.dtype)

    @pl.when(j >= n_norm_tiles)
    def _():
        o_ref[...] = y.astype(o_ref.dtype)


def _sb_inproj(x, ln_g, mod3, w, layer, gain_row, tm=1024, tn=1024):
    b, s, d = x.shape
    n = w.shape[2]
    return pl.pallas_call(
        functools.partial(_sb_inproj_kernel, n_norm_tiles=2 * d // tn),
        grid=(b, s // tm, n // tn),
        in_specs=[
            pl.BlockSpec((None, tm, d), lambda b, i, j: (b, i, 0)),
            pl.BlockSpec((1, d), lambda b, i, j: (0, 0)),
            *_mod_specs(d, 0, 1),
            pl.BlockSpec((None, d, tn), lambda b, i, j: (layer, 0, j)),
            pl.BlockSpec((1, tn), lambda b, i, j: (0, j)),
        ],
        out_specs=pl.BlockSpec((None, tm, tn), lambda b, i, j: (b, i, j)),
        out_shape=jax.ShapeDtypeStruct((b, s, n), BF16),
        scratch_shapes=[pltpu.VMEM((tm, d), BF16)],
        compiler_params=_cparams(("parallel", "parallel", "arbitrary")),
        name="sb_inproj",
    )(x, ln_g.reshape(1, d), mod3, mod3, w, gain_row)


def _out_gate_kernel(a_ref, w_ref, x_ref, gate_ref, o_ref):
    y = jnp.dot(a_ref[...], w_ref[...], preferred_element_type=F32)
    o_ref[...] = x_ref[...] + gate_ref[...] * y


def _out_gate(a, w, layer, x, mod3, gate_chunk, tm=1024, tn=1024):
    b, s, d = x.shape
    k = a.shape[2]
    nt = d // tn
    return pl.pallas_call(
        _out_gate_kernel,
        grid=(b, s // tm, nt),
        in_specs=[
            pl.BlockSpec((None, tm, k), lambda b, i, j: (b, i, 0)),
            pl.BlockSpec((None, k, tn), lambda b, i, j: (layer, 0, j)),
            pl.BlockSpec((None, tm, tn), lambda b, i, j: (b, i, j)),
            pl.BlockSpec((None, 1, tn), lambda b, i, j: (b, 0, gate_chunk * nt + j)),
        ],
        out_specs=pl.BlockSpec((None, tm, tn), lambda b, i, j: (b, i, j)),
        out_shape=jax.ShapeDtypeStruct((b, s, d), F32),
        compiler_params=_cparams(("parallel", "parallel", "parallel")),
        name="out_gate",
    )(a, w, x, mod3)


def _glu_gate_kernel(z_ref, wa_ref, wg_ref, x_ref, gate_ref, o_ref, zb_ref):
    @pl.when(pl.program_id(2) == 0)
    def _():
        zb_ref[...] = z_ref[...].astype(BF16)

    z = zb_ref[...]
    a = jnp.dot(z, wa_ref[...], preferred_element_type=F32)
    g = jnp.dot(z, wg_ref[...], preferred_element_type=F32)
    o_ref[...] = x_ref[...] + gate_ref[...] * (a * _sigmoid(g))


def _glu_gate(z, w_glu, layer, x, mod3, gate_chunk, tm=1024, tn=512):
    b, s, d = x.shape
    nt = d // tn
    return pl.pallas_call(
        _glu_gate_kernel,
        grid=(b, s // tm, nt),
        in_specs=[
            pl.BlockSpec((None, tm, d), lambda b, i, j: (b, i, 0)),
            pl.BlockSpec((None, d, tn), lambda b, i, j: (layer, 0, j)),
            pl.BlockSpec((None, d, tn), lambda b, i, j: (layer, 0, nt + j)),
            pl.BlockSpec((None, tm, tn), lambda b, i, j: (b, i, j)),
            pl.BlockSpec((None, 1, tn), lambda b, i, j: (b, 0, gate_chunk * nt + j)),
        ],
        out_specs=pl.BlockSpec((None, tm, tn), lambda b, i, j: (b, i, j)),
        out_shape=jax.ShapeDtypeStruct((b, s, d), F32),
        scratch_shapes=[pltpu.VMEM((tm, d), BF16)],
        compiler_params=_cparams(("parallel", "parallel", "arbitrary")),
        name="glu_gate",
    )(z, w_glu, w_glu, x, mod3)


def _mlp_kernel(x_ref, g_ref, sh_ref, sc_ref, w1_ref, w2_ref, gate_ref, o_ref, h_ref, acc_ref):
    j = pl.program_id(2)

    @pl.when(j == 0)
    def _():
        h_ref[...] = _rms_mod(x_ref, g_ref, sh_ref, sc_ref).astype(BF16)
        acc_ref[...] = jnp.zeros_like(acc_ref)

    a = jnp.maximum(jnp.dot(h_ref[...], w1_ref[...], preferred_element_type=F32), 0.0)
    acc_ref[...] += jnp.dot((a * a).astype(BF16), w2_ref[...], preferred_element_type=F32)

    @pl.when(j == pl.num_programs(2) - 1)
    def _():
        o_ref[...] = x_ref[...] + gate_ref[...] * acc_ref[...]


def _mlp(x, ln_g, mod3, w1, w2, layer, tm=512, tf=1024):
    b, s, d = x.shape
    f = w1.shape[2]
    return pl.pallas_call(
        _mlp_kernel,
        grid=(b, s // tm, f // tf),
        in_specs=[
            pl.BlockSpec((None, tm, d), lambda b, i, j: (b, i, 0)),
            pl.BlockSpec((1, d), lambda b, i, j: (0, 0)),
            *_mod_specs(d, 3, 4),
            pl.BlockSpec((None, d, tf), lambda b, i, j: (layer, 0, j)),
            pl.BlockSpec((None, tf, d), lambda b, i, j: (layer, j, 0)),
            pl.BlockSpec((None, 1, d), lambda b, i, j: (b, 0, 5)),
        ],
        out_specs=pl.BlockSpec((None, tm, d), lambda b, i, j: (b, i, 0)),
        out_shape=jax.ShapeDtypeStruct((b, s, d), F32),
        scratch_shapes=[pltpu.VMEM((tm, d), BF16), pltpu.VMEM((tm, d), F32)],
        compiler_params=_cparams(("parallel", "parallel", "arbitrary")),
        name="mlp",
    )(x, ln_g.reshape(1, d), mod3, mod3, w1, w2, mod3)


def _sb_attn_kernel(q_ref, k_ref, v_ref, later_ref, o_ref, z_scr, e_scr, *, tq, tk):
    i = pl.program_id(2)
    m = (i + 1) * (tq // tk)
    n_diag = tq // tk
    q = q_ref[...]
    lead =lax.broadcasted_iota(jnp.int32, (tq, tk), 0) - lax.broadcasted_iota(jnp.int32, (tq, tk), 1)
    sign = jnp.uint32(0x80000000)

    def key_start(n):
        return pl.multiple_of(jnp.maximum(m - 1 - n, 0) * tk, tk)

    def score(n):
        kb = k_ref[pl.ds(key_start(n), tk), :]
        return lax.dot_general(q, kb, (((1,), (1,)), ((), ())), preferred_element_type=F32)

    def log_weights(n, z, masked):
        neg_abs = pltpu.bitcast(pltpu.bitcast(z, jnp.uint32) | sign, F32)
        ls = jnp.minimum(z, 0.0) - jnp.log(1.0 + jnp.exp2(neg_abs)) * LOG2_E
        ln = ls - z
        if masked:
            causal = key_start(n) - i * tq < lead
            ln = jnp.where(causal, ln, 0.0)
            ls = jnp.where(causal, ls, NEG_BIG)
        ln = ln.astype(BF16)
        later = jnp.dot(ln, later_ref[...], preferred_element_type=F32)
        return ls + later, later[:, 0:1] + ln[:, 0:1].astype(F32)

    def accumulate(n, e, carry, acc):
        vb = v_ref[pl.ds(key_start(n), tk), :]
        w = jnp.exp2(e + carry).astype(BF16)
        return acc + jnp.dot(w, vb, preferred_element_type=F32)

    e0, tot0 = log_weights(0, score(0), True)
    e_scr[0] = e0
    z_scr[0] = score(1)

    def step(n, c):
        carry, tot, acc = c
        slot = n % 2
        z_scr[1 - slot] = score(n + 2)
        acc = accumulate(n, e_scr[slot], carry, acc)
        e_next, tot_next = log_weights(n + 1, z_scr[slot], True)
        e_scr[1 - slot] = e_next
        return carry + tot, tot_next, acc

    def pair(p, c):
        n = n_diag - 1 + 2 * p
        carry, tot, acc = c
        z_mid = score(n + 2)
        acc = accumulate(n, e_scr[1], carry, acc)
        e_mid, tot_mid = log_weights(n + 1, z_scr[1], False)
        carry = carry + tot
        z_scr[1] = score(n + 3)
        acc = accumulate(n + 1, e_mid, carry, acc)
        e_last, tot_last = log_weights(n + 2, z_mid, False)
        e_scr[1] = e_last
        return carry + tot_mid, tot_last, acc

    state = (jnp.zeros((tq, 1), F32), tot0, jnp.zeros((tq, HEAD_DIM), F32))
    assert n_diag % 2 == 0
    for n in range(n_diag - 1):
        state = step(n, state)
    carry, _, acc = lax.fori_loop(0, (m - n_diag) // 2, pair, state)
    acc = accumulate(m - 1, e_scr[1], carry, acc)
    o_ref[...] = acc.astype(o_ref.dtype)


def _sb_attention(qkv, n_heads, tq=512, tk=256):
    b, s, _ = qkv.shape
    pos = jnp.arange(tk)
    later_mat = (pos[:, None] > pos[None, :]).astype(BF16)
    return pl.pallas_call(
        functools.partial(_sb_attn_kernel, tq=tq, tk=tk),
        grid=(b, n_heads, s // tq),
        in_specs=[
            pl.BlockSpec((None, tq, HEAD_DIM), lambda b, h, i: (b, i, h)),
            pl.BlockSpec((None, s, HEAD_DIM), lambda b, h, i: (b, 0, n_heads + h)),
            pl.BlockSpec((None, s, HEAD_DIM), lambda b, h, i: (b, 0, 2 * n_heads + h)),
            pl.BlockSpec((tk, tk), lambda b, h, i: (0, 0)),
        ],
        out_specs=pl.BlockSpec((None, tq, HEAD_DIM), lambda b, h, i: (b, i, h)),
        out_shape=jax.ShapeDtypeStruct((b, s, n_heads * HEAD_DIM), BF16),
        scratch_shapes=[pltpu.VMEM((2, tq, tk), F32), pltpu.VMEM((2, tq, tk), F32)],
        compiler_params=_cparams(("parallel", "parallel", "parallel")),
        name="sb_attn",
    )(qkv, qkv, qkv, later_mat)


def _s5_tables(lam_re, lam_im, log_dt, b_re, b_im, c_re, c_im):
    g, p, gc = b_re.shape
    l = S5_CHUNK
    gpt = LANES // gc
    nt = g // gpt
    hi = lax.Precision.HIGHEST
    dt = jnp.exp(log_dt.astype(F32))[:, None]
    lr = lam_re.astype(F32)
    li = lam_im.astype(F32)
    mag = jnp.exp(lr * dt)
    ar = mag * jnp.cos(li * dt)
    ai = mag * jnp.sin(li * dt)
    den = lr * lr + li * li
    fr = ((ar - 1.0) * lr + ai * li) / den
    fi = (ai * lr - (ar - 1.0) * li) / den
    br_ = b_re.astype(F32)
    bi_ = b_im.astype(F32)
    bbr = fr[..., None] * br_ - fi[..., None] * bi_
    bbi = fr[..., None] * bi_ + fi[..., None] * br_
    cr = c_re.astype(F32)
    ci = c_im.astype(F32)
    n = jnp.arange(l + 1, dtype=F32)[:, None, None]
    pw_r = jnp.exp(n * (lr * dt)) * jnp.cos(n * (li * dt))
    pw_i = jnp.exp(n * (lr * dt)) * jnp.sin(n * (li * dt))
    bt_r = bbr.transpose(0, 2, 1)
    bt_i = bbi.transpose(0, 2, 1)
    ab_r = pw_r[:l, :, None, :] * bt_r - pw_i[:l, :, None, :] * bt_i
    ab_i = pw_r[:l, :, None, :] * bt_i + pw_i[:l, :, None, :] * bt_r

    def tile_rows(t):
        cols = t.shape[-1]
        return t.reshape(l, nt, gpt, gc, cols).transpose(1, 0, 2, 3, 4).reshape(nt, l * LANES, cols)

    wz = tile_rows(jnp.concatenate([ab_r[::-1], ab_i[::-1]], axis=-1))

    kern = (jnp.einsum('gcp,ngdp->gdnc', cr, ab_r, precision=hi)
            - jnp.einsum('gcp,ngdp->gdnc', ci, ab_i, precision=hi)).reshape(g, gc, l * gc)
    intra = tile_rows(jnp.stack([jnp.pad(kern[:, :, :(l - j) * gc], ((0, 0), (0, 0), (j * gc, 0)))
                                 for j in range(l)]))

    ct_r = jnp.tile(cr.transpose(0, 2, 1), (1, 1, l))
    ct_i = jnp.tile(ci.transpose(0, 2, 1), (1, 1, l))
    p1_r = jnp.repeat(pw_r[1:].transpose(1, 2, 0), gc, axis=-1)
    p1_i = jnp.repeat(pw_i[1:].transpose(1, 2, 0), gc, axis=-1)
    vr = (ct_r * p1_r - ct_i * p1_i).reshape(nt, gpt * p, l * gc)
    vi = (-(ct_r * p1_i + ct_i * p1_r)).reshape(nt, gpt * p, l * gc)
    wy = jnp.concatenate([intra, vr, vi], axis=1)

    alr = pw_r[l].reshape(nt, 1, gpt * p)
    ali = pw_i[l].reshape(nt, 1, gpt * p)
    return wz.astype(BF16), wy.astype(BF16), alr, ali


def _spread_groups(src_ref, dst_ref, row_group_div, col_unit):
    n_src = src_ref.shape[1]
    n_dst = dst_ref.shape[1]
    groups = n_dst // n_src
    rc = 256
    sr = lax.broadcasted_iota(jnp.int32, (n_src, n_dst), 0)
    dc = lax.broadcasted_iota(jnp.int32, (n_src, n_dst), 1)
    spread = ((sr // col_unit == dc // (col_unit * groups)) & (sr % col_unit == dc % col_unit))
    spread = jnp.where(spread, 1.0, 0.0).astype(BF16)
    rr = lax.broadcasted_iota(jnp.int32, (rc, n_dst), 0)
    cg = (lax.broadcasted_iota(jnp.int32, (rc, n_dst), 1) // col_unit) % groups
    for r0 in range(0, src_ref.shape[0], rc):
        wide = jnp.dot(src_ref[r0:r0 + rc, :], spread, preferred_element_type=F32)
        keep = ((r0 + rr) // row_group_div) % groups == cg
        dst_ref[r0:r0 + rc, :] = jnp.where(keep, wide, 0.0).astype(dst_ref.dtype)


def _s5_core_kernel(ut_ref, wzc_ref, wyc_ref, alr_ref, ali_ref, d_ref, o_ref,
                    wz_ref, wy_ref, zr_ref, zi_ref, hr_ref, hi_ref, *, l, nk, gc, p):
    @pl.when(pl.program_id(1) == 0)
    def _():
        n_in = l * LANES
        _spread_groups(wzc_ref, wz_ref, gc, p)
        _spread_groups(wyc_ref.at[:n_in], wy_ref.at[:n_in], gc, gc)
        _spread_groups(wyc_ref.at[n_in:], wy_ref.at[n_in:], p, gc)

    us = [ut_ref[pl.ds(j, nk, stride=l), :] for j in range(l)]
    ub = jnp.concatenate([u.astype(BF16) for u in us], axis=1)
    z = jnp.dot(ub, wz_ref[...], preferred_element_type=F32)
    half = z.shape[1] // 2
    zr_ref[...] = z[:, :half]
    zi_ref[...] = z[:, half:]
    alr = alr_ref[...]
    ali = ali_ref[...]
    rows = lax.broadcasted_iota(jnp.int32, (SUBLANES, half), 0)

    def body(kb, c):
        hr, hi = c
        base = pl.multiple_of(kb * SUBLANES, SUBLANES)
        zr = zr_ref[pl.ds(base, SUBLANES), :]
        zi = zi_ref[pl.ds(base, SUBLANES), :]
        out_r = jnp.zeros((SUBLANES, half), F32)
        out_i = jnp.zeros((SUBLANES, half), F32)
        for r in range(SUBLANES):
            out_r = jnp.where(rows == r, hr, out_r)
            out_i = jnp.where(rows == r, hi, out_i)
            hr, hi = (alr * hr - ali * hi + zr[r:r + 1, :], alr * hi + ali * hr + zi[r:r + 1, :])
        hr_ref[pl.ds(base, SUBLANES), :] = out_r
        hi_ref[pl.ds(base, SUBLANES), :] = out_i
        return hr, hi

    zero = jnp.zeros((1, half), F32)
    lax.fori_loop(0, nk // SUBLANES, body, (zero, zero))

    lhs = jnp.concatenate([ub, hr_ref[...].astype(BF16), hi_ref[...].astype(BF16)], axis=1)
    y = jnp.dot(lhs, wy_ref[...], preferred_element_type=F32)
    d = d_ref[...]
    for i in range(l):
        yi = y[:, i * LANES:(i + 1) * LANES] + d * us[i]
        o_ref[pl.ds(i, nk, stride=l), :] = _gelu_tanh(yi).astype(o_ref.dtype)


def _s5_core(u, tables, d_skip):
    b, s, d = u.shape
    l = S5_CHUNK
    nk = s // l
    nt = d // LANES
    wz, wy, alr, ali = tables
    half = alr.shape[2]
    gpt = LANES // S5_GROUP
    return pl.pallas_call(
        functools.partial(_s5_core_kernel, l=l, nk=nk, gc=S5_GROUP, p=half // gpt),
        grid=(nt, b),
        in_specs=[
            pl.BlockSpec((None, s, LANES), lambda t, b: (b, 0, t)),
            pl.BlockSpec((None,) + wz.shape[1:], lambda t, b: (t, 0, 0)),
            pl.BlockSpec((None,) + wy.shape[1:], lambda t, b: (t, 0, 0)),
            pl.BlockSpec((None, 1, half), lambda t, b: (t, 0, 0)),
            pl.BlockSpec((None, 1, half), lambda t, b: (t, 0, 0)),
            pl.BlockSpec((None, 1, LANES), lambda t, b: (t, 0, 0)),
        ],
        out_specs=pl.BlockSpec((None, s, LANES), lambda t, b: (b, 0, t)),
        out_shape=jax.ShapeDtypeStruct((b, s, d), F32),
        scratch_shapes=[pltpu.VMEM((l * LANES, 2 * half), BF16),
                        pltpu.VMEM((l * LANES + 2 * half, l * LANES), BF16)] + [pltpu.VMEM((nk, half), F32)] * 4,
        compiler_params=_cparams(("arbitrary", "arbitrary")),
        name="s5_core",
    )(u, wz, wy, alr, ali, d_skip.astype(F32).reshape(nt, 1, LANES))


def _rope_tab_kernel(pos_ref, f_head_ref, f_idx_ref, ch_ref, sh_ref, ci_ref, si_ref):
    p = pos_ref[...].astype(F32)
    lane = lax.broadcasted_iota(jnp.int32, ch_ref.shape, 1)
    a = p * f_head_ref[...]
    ch_ref[...] = jnp.cos(a)
    sh_ref[...] = jnp.where(lane < HEAD_DIM // 2, -jnp.sin(a), jnp.sin(a))
    a = p * f_idx_ref[...]
    ci_ref[...] = jnp.cos(a)
    si_ref[...] = jnp.where(lane % IDX_DIM < IDX_DIM // 2, -jnp.sin(a), jnp.sin(a))


def _rope_tables(positions, tm=1024):
    b, s = positions.shape

    def inv_freq(dim):
        return ROPE_THETA ** (-jnp.arange(0, dim, 2, dtype=F32) / dim)

    f_head = jnp.tile(inv_freq(HEAD_DIM), 2).reshape(1, LANES)
    f_idx = jnp.tile(inv_freq(IDX_DIM), 2 * LANES // IDX_DIM).reshape(1, LANES)
    row_spec = pl.BlockSpec((None, tm, LANES), lambda b, i: (b, i, 0))
    return pl.pallas_call(
        _rope_tab_kernel,
        grid=(b, s // tm),
        in_specs=[
            pl.BlockSpec((None, tm, 1), lambda b, i: (b, i, 0)),
            pl.BlockSpec((1, LANES), lambda b, i: (0, 0)),
            pl.BlockSpec((1, LANES), lambda b, i: (0, 0)),
        ],
        out_specs=[row_spec] * 4,
        out_shape=[jax.ShapeDtypeStruct((b, s, LANES), F32)] * 4,
        compiler_params=_cparams(("parallel", "parallel")),
        name="rope_tables",
    )(positions.reshape(b, s, 1), f_head, f_idx)


def _rope_head(y, cos, sin_signed):
    return y * cos + pltpu.roll(y, HEAD_DIM // 2, axis=1) * sin_signed


def _rope_idx(y, cos, sin_signed):
    lane = lax.broadcasted_iota(jnp.int32, y.shape, 1)
    half = IDX_DIM // 2
    partner = jnp.where(lane % IDX_DIM < half, pltpu.roll(y, LANES - half, axis=1), pltpu.roll(y, half, axis=1))
    return y * cos + partner * sin_signed


def _dsa_inproj_kernel(x_ref, g_ref, sh_ref, sc_ref, w_ref, gain_ref, ch_ref, shd_ref, ci_ref, si_ref,
                       o_ref, tail_ref, h_ref, *, n_qk, n_v, n_qi):
    j = pl.program_id(2)

    @pl.when(j == 0)
    def _():
        h_ref[...] = _rms_mod(x_ref, g_ref, sh_ref, sc_ref).astype(BF16)

    y = jnp.dot(h_ref[...], w_ref[...], preferred_element_type=F32)
    slices = [slice(hh * LANES, (hh + 1) * LANES) for hh in range(y.shape[1] // LANES)]

    @pl.when(j < n_qk)
    def _():
        for sl in slices:
            yn = _head_rmsnorm(y[:, sl], gain_ref[:, sl])
            o_ref[:, sl] = _rope_head(yn, ch_ref[...], shd_ref[...]).astype(o_ref.dtype)

    @pl.when((j >= n_qk) & (j < n_qk + n_v))
    def _():
        o_ref[...] = y.astype(o_ref.dtype)

    @pl.when((j >= n_qk + n_v) & (j < n_qk + n_v + n_qi))
    def _():
        for sl in slices:
            o_ref[:, sl] = (_rope_idx(y[:, sl], ci_ref[...], si_ref[...]) * IDX_DIM ** -0.5).astype(o_ref.dtype)

    @pl.when(j == n_qk + n_v + n_qi)
    def _():
        t = y[:, :LANES]
        lane = lax.broadcasted_iota(jnp.int32, t.shape, 1)
        tail_ref[...] = jnp.where(lane < IDX_DIM, _rope_idx(t, ci_ref[...], si_ref[...]), t * IDX_HEADS ** -0.5)


def _dsa_inproj(x, ln_g, mod3, w_pad, gain_row, tabs, n_heads, tm=1024, tn=512):
    b, s, d = x.shape
    dm = n_heads * HEAD_DIM
    n_qk, n_v, n_qi = 2 * dm // tn, dm // tn, IDX_HEADS * IDX_DIM // tn
    n_main = n_qk + n_v + n_qi
    tab_spec = pl.BlockSpec((None, tm, LANES), lambda b, i, j: (b, i, 0))
    return pl.pallas_call(
        functools.partial(_dsa_inproj_kernel, n_qk=n_qk, n_v=n_v, n_qi=n_qi),
        grid=(b, s // tm, n_main + 1),
        in_specs=[
            pl.BlockSpec((None, tm, d), lambda b, i, j: (b, i, 0)),
            pl.BlockSpec((1, d), lambda b, i, j: (0, 0)),
            *_mod_specs(d, 0, 1),
            pl.BlockSpec((d, tn), lambda b, i, j: (0, j)),
            pl.BlockSpec((1, tn), lambda b, i, j: (0, j)),
            tab_spec, tab_spec, tab_spec, tab_spec,
        ],
        out_specs=[
            pl.BlockSpec((None, tm, tn), lambda b, i, j: (b, i, jnp.minimum(j, n_main - 1))),
            pl.BlockSpec((None, tm, LANES), lambda b, i, j: (b, i, 0)),
        ],
        out_shape=[jax.ShapeDtypeStruct((b, s, n_main * tn), BF16),
                   jax.ShapeDtypeStruct((b, s, LANES), F32)],
        scratch_shapes=[pltpu.VMEM((tm, d), BF16)],
        compiler_params=_cparams(("parallel", "parallel", "arbitrary")),
        name="dsa_inproj",
    )(x, ln_g.reshape(1, d), mod3, mod3, w_pad, gain_row, *tabs)


def _dsa_core_kernel(q_ref, k_ref, v_ref, qi_ref, ki_ref, wi_ref, o_ref, key_ref, bias_ref, s_scr,
                     *, t, topk, idx_bits, strip):
    i = pl.program_id(1)
    h = pl.program_id(2)
    nk = i + 1
    kf = float(topk)

    def chunk(c):
        return pl.ds(pl.multiple_of(c * t, t), t)

    @pl.when(h == 0)
    def _():
        qi = qi_ref[...]
        wi = wi_ref[...]
        row = lax.broadcasted_iota(jnp.int32, (t, t), 0)
        col = lax.broadcasted_iota(jnp.int32, (t, t), 1)

        def score_body(c, _):
            ki = ki_ref[chunk(c), :][:, :IDX_DIM].astype(BF16)
            score = jnp.zeros((t, t), F32)
            for hh in range(IDX_HEADS):
                rel = lax.dot_general(qi[:, hh * IDX_DIM:(hh + 1) * IDX_DIM], ki, (((1,), (1,)), ((), ())),
                                      preferred_element_type=F32)
                score = score + wi[:, IDX_DIM + hh:IDX_DIM + hh + 1] * jnp.maximum(rel, 0.0)
            bits = pltpu.bitcast(score + 0.0, jnp.int32)
            key = jnp.where(bits < 0, bits ^ jnp.int32(0x7FFFFFFF), bits)
            key = jnp.where((c - i) * t + col <= row, key, jnp.int32(INT_MIN))
            key_ref[:, chunk(c)] = key
            return 0

        lax.fori_loop(0, nk, score_body, 0)

        lane = lax.broadcasted_iota(jnp.int32, (strip, LANES), 1)

        def count(pred, *row_args):
            counts = []
            for r in range(t // strip):
                rows = slice(r * strip, (r + 1) * strip)
                args = [jnp.broadcast_to(a[rows], (strip, LANES)) for a in row_args]

                def body(c, acc, rows=rows, args=args):
                    for s in range(t // LANES):
                        start = pl.multiple_of(c * t + s * LANES, LANES)
                        hit = pred(key_ref[rows, pl.ds(start, LANES)], start + lane, *args)
                        acc = acc + jnp.where(hit, 1.0, 0.0)
                    return acc

                counts.append(lax.fori_loop(0, nk, body, jnp.zeros((strip, LANES), F32)))
            return jnp.sum(jnp.concatenate(counts, axis=0), axis=1, keepdims=True)

        def bisect(n, thr):
            cand = thr + lax.shift_left(jnp.int32(1), 31 - n)
            return jnp.where(count(lambda key, pos, cd: key >= cd, cand) >= kf, cand, thr)

        thr = lax.fori_loop(0, 32, bisect, jnp.full((t, 1), INT_MIN, jnp.int32))
        thr = jnp.maximum(thr, jnp.int32(INT_MIN + 1))
        n_ge = count(lambda key, pos, th: key >= th, thr)

        def tie_cut(_):
            need = kf - count(lambda key, pos, th: key > th, thr)
            def step(n, cut):
                cand = cut + lax.shift_left(jnp.int32(1), idx_bits - 1 - n)
                below = count(lambda key, pos, th, cd: (key == th) & (pos < cd), thr, cand)
                return jnp.where(below < need, cand, cut)
            return lax.fori_loop(0, idx_bits, step, jnp.zeros((t, 1), jnp.int32))

        cut = lax.cond(jnp.max(n_ge) > kf, tie_cut, lambda _: jnp.full((t, 1), 2 ** idx_bits - 1, jnp.int32), 0)

        def bias_body(c, _):
            key = key_ref[:, chunk(c)]
            sel = (key > thr) | ((key == thr) & (c * t + col <= cut))
            bias_ref[:, chunk(c)] = jnp.where(sel, 0.0, NEG_BIG)
            return 0

        lax.fori_loop(0, nk, bias_body, 0)

    q = q_ref[...]

    def logits(c):
        kc = k_ref[chunk(jnp.minimum(c, nk - 1)), :]
        return lax.dot_general(q, kc, (((1,), (1,)), ((), ())), preferred_element_type=F32)

    def visit(c, s, carry):
        m, l, acc = carry
        s = s + bias_ref[:, chunk(c)]
        m_new = jnp.maximum(m, jnp.max(s, axis=1, keepdims=True))
        alpha = jnp.exp2(m - m_new)
        p = jnp.exp2(s - m_new)
        l = alpha * l + jnp.sum(p, axis=1, keepdims=True)
        acc = alpha * acc + jnp.dot(p.astype(BF16), v_ref[chunk(c), :], preferred_element_type=F32)
        return m_new, l, acc

    s_scr[...] = logits(0)

    def pair(n, carry):
        s_mid = logits(2 * n + 1)
        carry = visit(2 * n, s_scr[...], carry)
        s_scr[...] = logits(2 * n + 2)
        return visit(2 * n + 1, s_mid, carry)

    m0 = jnp.full((t, 1), NEG_BIG, F32)
    l0 = jnp.zeros((t, 1), F32)
    acc0 = jnp.zeros((t, HEAD_DIM), F32)
    carry = lax.fori_loop(0, lax.shift_right_logical(nk, 1), pair, (m0, l0, acc0))
    _, l, acc = lax.cond((nk & 1) == 1, lambda c: visit(nk - 1, s_scr[...], c), lambda c: c, carry)
    o_ref[...] = (acc / l).astype(o_ref.dtype)


def _dsa_core(main, tail, n_heads, t=512):
    b, s, _ = main.shape
    topk = min(DSA_TOPK_MAX, s // 4)
    idx_bits = max(1, (s - 1).bit_length())
    qi_block = (3 * n_heads * HEAD_DIM) // (IDX_HEADS * IDX_DIM)
    return pl.pallas_call(
        functools.partial(_dsa_core_kernel, t=t, topk=topk, idx_bits=idx_bits, strip=min(t, 128)),
        grid=(b, s // t, n_heads),
        in_specs=[
            pl.BlockSpec((None, t, HEAD_DIM), lambda b, i, h: (b, i, h)),
            pl.BlockSpec((None, s, HEAD_DIM), lambda b, i, h: (b, 0, n_heads + h)),
            pl.BlockSpec((None, s, HEAD_DIM), lambda b, i, h: (b, 0, 2 * n_heads + h)),
            pl.BlockSpec((None, t, IDX_HEADS * IDX_DIM), lambda b, i, h: (b, i, qi_block)),
            pl.BlockSpec((None, s, LANES), lambda b, i, h: (b, 0, 0)),
            pl.BlockSpec((None, t, LANES), lambda b, i, h: (b, i, 0)),
        ],
        out_specs=pl.BlockSpec((None, t, HEAD_DIM), lambda b, i, h: (b, i, h)),
        out_shape=jax.ShapeDtypeStruct((b, s, n_heads * HEAD_DIM), BF16),
        scratch_shapes=[pltpu.VMEM((t, s), jnp.int32), pltpu.VMEM((t, s), F32), pltpu.VMEM((t, t), F32)],
        compiler_params=_cparams(("parallel", "parallel", "arbitrary")),
        name="dsa_core",
    )(main, main, main, main, tail, tail)


def kernel(x, c, positions, ln1_g, ln2_g, ada_w, ada_b, mlp_w1, mlp_w2, sb_w_in, sb_q_gain, sb_k_gain, sb_w_out, s5_w_in, s5_lambda_re, s5_lambda_im, s5_log_dt, s5_b_re, s5_b_im, s5_c_re, s5_c_im, s5_d, s5_w_glu, dsa_w_in, dsa_q_gain, dsa_k_gain, dsa_w_out):
    depth = ada_w.shape[0]
    b, s, d = x.shape
    n_heads = d // HEAD_DIM
    scale = HEAD_DIM ** -0.5

    mod = _ada_mod(c, ada_w, ada_b)
    rope_tabs = _rope_tables(positions) if depth > 2 else None
    mlp_w1, mlp_w2, sb_w_in, sb_w_out, s5_w_in, s5_w_glu, dsa_w_out = (
        w.astype(BF16) for w in (mlp_w1, mlp_w2, sb_w_in, sb_w_out, s5_w_in, s5_w_glu, dsa_w_out))

    counts = [0, 0, 0]
    for i in range(depth):
        mod3 = mod[i].reshape(SUBLANES, 1, 6 * d)
        kind = i % N_MIXERS
        j = counts[kind]
        counts[kind] += 1
        if kind == 0:
            gain = jnp.concatenate([jnp.tile(sb_q_gain[j].astype(F32) * (scale * LOG2_E), n_heads),
                                    jnp.tile(sb_k_gain[j].astype(F32), n_heads),
                                    jnp.ones((d,), F32)]).reshape(1, 3 * d)
            qkv = _sb_inproj(x, ln1_g[i], mod3, sb_w_in, j, gain)
            o = _sb_attention(qkv, n_heads)
            x = _out_gate(o, sb_w_out, j, x, mod3, 2)
        elif kind == 1:
            u = _nm_matmul(x, ln1_g[i], mod3, s5_w_in, j, F32)
            tables = _s5_tables(s5_lambda_re[j], s5_lambda_im[j], s5_log_dt[j], s5_b_re[j], s5_b_im[j],
                                s5_c_re[j], s5_c_im[j])
            z = _s5_core(u, tables, s5_d[j])
            x = _glu_gate(z, s5_w_glu, j, x, mod3, 2)
        else:
            tn = 512
            n_in = dsa_w_in.shape[2]
            n_main = 3 * d + IDX_HEADS * IDX_DIM
            w_pad = jnp.zeros((d, n_main + tn), BF16).at[:, :n_in].set(dsa_w_in[j].astype(BF16))
            gain = jnp.zeros((1, n_main + tn), F32)
            gain = gain.at[0, :d].set(jnp.tile(dsa_q_gain[j].astype(F32) * (scale * LOG2_E), n_heads))
            gain = gain.at[0, d:2 * d].set(jnp.tile(dsa_k_gain[j].astype(F32), n_heads))
            main, tail = _dsa_inproj(x, ln1_g[i], mod3, w_pad, gain, rope_tabs, n_heads, tn=tn)
            o = _dsa_core(main, tail, n_heads)
            x = _out_gate(o, dsa_w_out, j, x, mod3, 2)
        x = _mlp(x, ln2_g[i], mod3, mlp_w1, mlp_w2, i)
    return x
```

```python
import functools
import math

import jax
import jax.numpy as jnp
from jax import lax
from jax.experimental import pallas as pl
from jax.experimental.pallas import tpu as pltpu

F32 = jnp.float32
BF16 = jnp.bfloat16

N_MIXERS = 3
HEAD_DIM = 128
ROPE_THETA = 10000.0
EPS = 1e-6
S5_GROUP = 16
S5_STATE = 64
IDX_HEADS = 16
IDX_DIM = 64
DSA_TOPK_MAX = 256

LANES = 128
SUBLANES = 8
VMEM_LIMIT_BYTES = 56 * 1024 * 1024

S5_CHUNK = 16
NEG_BIG = -1e30
INT_MIN = -(2 ** 31)
LOG2_E = 1.4426950408889634


def _cparams(sem):
    return pltpu.CompilerParams(dimension_semantics=sem, vmem_limit_bytes=VMEM_LIMIT_BYTES)


def _sigmoid(x):
    return 1.0 / (1.0 + jnp.exp(-x))


def _gelu_tanh(x):
    c = math.sqrt(2.0 / math.pi)
    return 0.5 * x * (1.0 + jnp.tanh(c * (x + 0.044715 * (x * x * x))))


def _rms_mod(x_ref, g_ref, sh_ref, sc_ref):
    x = x_ref[...]
    ms = jnp.mean(x * x, axis=-1, keepdims=True)
    y = x * lax.rsqrt(ms + EPS) * g_ref[...]
    return y * (1.0 + sc_ref[...]) + sh_ref[...]


def _head_rmsnorm(y, gain):
    ms = jnp.mean(y * y, axis=-1, keepdims=True)
    return y * lax.rsqrt(ms + EPS) * gain


def _ada_kernel(c_ref, w_ref, b_ref, o_ref):
    c = c_ref[...]
    cond = (c * _sigmoid(c)).astype(BF16)
    o_ref[...] = jnp.dot(cond, w_ref[...].astype(BF16), preferred_element_type=F32) + b_ref[...]


def _ada_mod(c, ada_w, ada_b):
    depth, d, n = ada_w.shape
    b = c.shape[0]
    c8 = jnp.zeros((SUBLANES, d), F32).at[:b].set(c)
    tn = 1024
    return pl.pallas_call(
        _ada_kernel,
        grid=(depth, n // tn),
        in_specs=[
            pl.BlockSpec((SUBLANES, d), lambda l, j: (0, 0)),
            pl.BlockSpec((None, d, tn), lambda l, j: (l, 0, j)),
            pl.BlockSpec((None, 1, tn), lambda l, j: (l, 0, j)),
        ],
        out_specs=pl.BlockSpec((None, SUBLANES, tn), lambda l, j: (l, 0, j)),
        out_shape=jax.ShapeDtypeStruct((depth, SUBLANES, n), F32),
        compiler_params=_cparams(("parallel", "parallel")),
        name="ada_mod",
    )(c8, ada_w, ada_b.reshape(depth, 1, n))


def _mod_specs(d, shift_chunk, scale_chunk):
    return [
        pl.BlockSpec((None, 1, d), lambda b, i, j: (b, 0, shift_chunk)),
        pl.BlockSpec((None, 1, d), lambda b, i, j: (b, 0, scale_chunk)),
    ]


def _nm_matmul_kernel(x_ref, g_ref, sh_ref, sc_ref, w_ref, o_ref, h_ref):
    @pl.when(pl.program_id(2) == 0)
    def _():
        h_ref[...] = _rms_mod(x_ref, g_ref, sh_ref, sc_ref).astype(BF16)

    o_ref[...] = jnp.dot(h_ref[...], w_ref[...], preferred_element_type=F32).astype(o_ref.dtype)


def _nm_matmul(x, ln_g, mod3, w, layer, out_dtype, tm=1024, tn=1024):
    b, s, d = x.shape
    n = w.shape[2]
    return pl.pallas_call(
        _nm_matmul_kernel,
        grid=(b, s // tm, n // tn),
        in_specs=[
            pl.BlockSpec((None, tm, d), lambda b, i, j: (b, i, 0)),
            pl.BlockSpec((1, d), lambda b, i, j: (0, 0)),
            *_mod_specs(d, 0, 1),
            pl.BlockSpec((None, d, tn), lambda b, i, j: (layer, 0, j)),
        ],
        out_specs=pl.BlockSpec((None, tm, tn), lambda b, i, j: (b, i, j)),
        out_shape=jax.ShapeDtypeStruct((b, s, n), out_dtype),
        scratch_shapes=[pltpu.VMEM((tm, d), BF16)],
        compiler_params=_cparams(("parallel", "parallel", "arbitrary")),
        name="nm_matmul",
    )(x, ln_g.reshape(1, d), mod3, mod3, w)


def _sb_inproj_kernel(x_ref, g_ref, sh_ref, sc_ref, w_ref, gain_ref, o_ref, h_ref, *, n_norm_tiles):
    j = pl.program_id(2)

    @pl.when(j == 0)
    def _():
        h_ref[...] = _rms_mod(x_ref, g_ref, sh_ref, sc_ref).astype(BF16)

    y = jnp.dot(h_ref[...], w_ref[...], preferred_element_type=F32)

    @pl.when(j < n_norm_tiles)
    def _():
        for hh in range(y.shape[1] // HEAD_DIM):
            sl = slice(hh * HEAD_DIM, (hh + 1) * HEAD_DIM)
            o_ref[:, sl] = _head_rmsnorm(y[:, sl], gain_ref[:, sl]).astype(o_ref.dtype)

    @pl.when(j >= n_norm_tiles)
    def _():
        o_ref[...] = y.astype(o_ref.dtype)


def _sb_inproj(x, ln_g, mod3, w, layer, gain_row, tm=1024, tn=1024):
    b, s, d = x.shape
    n = w.shape[2]
    return pl.pallas_call(
        functools.partial(_sb_inproj_kernel, n_norm_tiles=2 * d // tn),
        grid=(b, s // tm, n // tn),
        in_specs=[
            pl.BlockSpec((None, tm, d), lambda b, i, j: (b, i, 0)),
            pl.BlockSpec((1, d), lambda b, i, j: (0, 0)),
            *_mod_specs(d, 0, 1),
            pl.BlockSpec((None, d, tn), lambda b, i, j: (layer, 0, j)),
            pl.BlockSpec((1, tn), lambda b, i, j: (0, j)),
        ],
        out_specs=pl.BlockSpec((None, tm, tn), lambda b, i, j: (b, i, j)),
        out_shape=jax.ShapeDtypeStruct((b, s, n), BF16),
        scratch_shapes=[pltpu.VMEM((tm, d), BF16)],
        compiler_params=_cparams(("parallel", "parallel", "arbitrary")),
        name="sb_inproj",
    )(x, ln_g.reshape(1, d), mod3, mod3, w, gain_row)


def _out_gate_kernel(a_ref, w_ref, x_ref, gate_ref, o_ref):
    y = jnp.dot(a_ref[...], w_ref[...], preferred_element_type=F32)
    o_ref[...] = x_ref[...] + gate_ref[...] * y


def _out_gate(a, w, layer, x, mod3, gate_chunk, tm=1024, tn=1024):
    b, s, d = x.shape
    k = a.shape[2]
    nt = d // tn
    return pl.pallas_call(
        _out_gate_kernel,
        grid=(b, s // tm, nt),
        in_specs=[
            pl.BlockSpec((None, tm, k), lambda b, i, j: (b, i, 0)),
            pl.BlockSpec((None, k, tn), lambda b, i, j: (layer, 0, j)),
            pl.BlockSpec((None, tm, tn), lambda b, i, j: (b, i, j)),
            pl.BlockSpec((None, 1, tn), lambda b, i, j: (b, 0, gate_chunk * nt + j)),
        ],
        out_specs=pl.BlockSpec((None, tm, tn), lambda b, i, j: (b, i, j)),
        out_shape=jax.ShapeDtypeStruct((b, s, d), F32),
        compiler_params=_cparams(("parallel", "parallel", "parallel")),
        name="out_gate",
    )(a, w, x, mod3)


def _glu_gate_kernel(z_ref, wa_ref, wg_ref, x_ref, gate_ref, o_ref, zb_ref):
    @pl.when(pl.program_id(2) == 0)
    def _():
        zb_ref[...] = z_ref[...].astype(BF16)

    z = zb_ref[...]
    a = jnp.dot(z, wa_ref[...], preferred_element_type=F32)
    g = jnp.dot(z, wg_ref[...], preferred_element_type=F32)
    o_ref[...] = x_ref[...] + gate_ref[...] * (a * _sigmoid(g))


def _glu_gate(z, w_glu, layer, x, mod3, gate_chunk, tm=1024, tn=512):
    b, s, d = x.shape
    nt = d // tn
    return pl.pallas_call(
        _glu_gate_kernel,
        grid=(b, s // tm, nt),
        in_specs=[
            pl.BlockSpec((None, tm, d), lambda b, i, j: (b, i, 0)),
            pl.BlockSpec((None, d, tn), lambda b, i, j: (layer, 0, j)),
            pl.BlockSpec((None, d, tn), lambda b, i, j: (layer, 0, nt + j)),
            pl.BlockSpec((None, tm, tn), lambda b, i, j: (b, i, j)),
            pl.BlockSpec((None, 1, tn), lambda b, i, j: (b, 0, gate_chunk * nt + j)),
        ],
        out_specs=pl.BlockSpec((None, tm, tn), lambda b, i, j: (b, i, j)),
        out_shape=jax.ShapeDtypeStruct((b, s, d), F32),
        scratch_shapes=[pltpu.VMEM((tm, d), BF16)],
        compiler_params=_cparams(("parallel", "parallel", "arbitrary")),
        name="glu_gate",
    )(z, w_glu, w_glu, x, mod3)


def _mlp_kernel(x_ref, g_ref, sh_ref, sc_ref, w1_ref, w2_ref, gate_ref, o_ref, h_ref, acc_ref):
    j = pl.program_id(2)

    @pl.when(j == 0)
    def _():
        h_ref[...] = _rms_mod(x_ref, g_ref, sh_ref, sc_ref).astype(BF16)
        acc_ref[...] = jnp.zeros_like(acc_ref)

    a = jnp.maximum(jnp.dot(h_ref[...], w1_ref[...], preferred_element_type=F32), 0.0)
    acc_ref[...] += jnp.dot((a * a).astype(BF16), w2_ref[...], preferred_element_type=F32)

    @pl.when(j == pl.num_programs(2) - 1)
    def _():
        o_ref[...] = x_ref[...] + gate_ref[...] * acc_ref[...]


def _mlp(x, ln_g, mod3, w1, w2, layer, tm=512, tf=1024):
    b, s, d = x.shape
    f = w1.shape[2]
    return pl.pallas_call(
        _mlp_kernel,
        grid=(b, s // tm, f // tf),
        in_specs=[
            pl.BlockSpec((None, tm, d), lambda b, i, j: (b, i, 0)),
            pl.BlockSpec((1, d), lambda b, i, j: (0, 0)),
            *_mod_specs(d, 3, 4),
            pl.BlockSpec((None, d, tf), lambda b, i, j: (layer, 0, j)),
            pl.BlockSpec((None, tf, d), lambda b, i, j: (layer, j, 0)),
            pl.BlockSpec((None, 1, d), lambda b, i, j: (b, 0, 5)),
        ],
        out_specs=pl.BlockSpec((None, tm, d), lambda b, i, j: (b, i, 0)),
        out_shape=jax.ShapeDtypeStruct((b, s, d), F32),
        scratch_shapes=[pltpu.VMEM((tm, d), BF16), pltpu.VMEM((tm, d), F32)],
        compiler_params=_cparams(("parallel", "parallel", "arbitrary")),
        name="mlp",
    )(x, ln_g.reshape(1, d), mod3, mod3, w1, w2, mod3)


def _sb_attn_kernel(q_ref, k_ref, vt_ref, later_ref, o_ref, z_scr, e_scr, *, tq, tk):
    i = pl.program_id(2)
    m = (i + 1) * (tq // tk)
    n_diag = tq // tk
    q = q_ref[...]
    lead = lax.broadcasted_iota(jnp.int32, (tk, tq), 1) - lax.broadcasted_iota(jnp.int32, (tk, tq), 0)
    sign = jnp.uint32(0x80000000)

    def key_start(n):
        return pl.multiple_of(jnp.maximum(m - 1 - n, 0) * tk, tk)

    def score(n):
        kb = k_ref[pl.ds(key_start(n), tk), :]
        return lax.dot_general(kb, q, (((1,), (1,)), ((), ())), preferred_element_type=F32)

    def log_weights(n, z, masked):
        neg_abs = pltpu.bitcast(pltpu.bitcast(z, jnp.uint32) | sign, F32)
        ls = jnp.minimum(z, 0.0) - jnp.log(1.0 + jnp.exp2(neg_abs)) * LOG2_E
        ln = ls - z
        if masked:
            causal = key_start(n) - i * tq < lead
            ln = jnp.where(causal, ln, 0.0)
            ls = jnp.where(causal, ls, NEG_BIG)
        ln = ln.astype(BF16)
        later = jnp.dot(later_ref[...], ln, preferred_element_type=F32)
        return ls + later, later[0:1, :] + ln[0:1, :].astype(F32)

    def accumulate(n, e, carry, acc):
        vt = vt_ref[:, pl.ds(key_start(n), tk)]
        w = jnp.exp2(e + carry).astype(BF16)
        return acc + jnp.dot(vt, w, preferred_element_type=F32)

    e0, tot0 = log_weights(0, score(0), True)
    e_scr[0] = e0
    z_scr[0] = score(1)

    def step(n, c):
        carry, tot, acc = c
        slot = n % 2
        z_scr[1 - slot] = score(n + 2)
        acc = accumulate(n, e_scr[slot], carry, acc)
        e_next, tot_next = log_weights(n + 1, z_scr[slot], True)
        e_scr[1 - slot] = e_next
        return carry + tot, tot_next, acc

    def pair(p, c):
        n = n_diag - 1 + 2 * p
        carry, tot, acc = c
        z_mid = score(n + 2)
        acc = accumulate(n, e_scr[1], carry, acc)
        e_mid, tot_mid = log_weights(n + 1, z_scr[1], False)
        carry = carry + tot
        z_scr[1] = score(n + 3)
        acc = accumulate(n + 1, e_mid, carry, acc)
        e_last, tot_last = log_weights(n + 2, z_mid, False)
        e_scr[1] = e_last
        return carry + tot_mid, tot_last, acc

    state = (jnp.zeros((1, tq), F32), tot0, jnp.zeros((HEAD_DIM, tq), F32))
    assert n_diag % 2 == 0
    for n in range(n_diag - 1):
        state = step(n, state)
    n_pairs = (i * n_diag) // 2
    n_quads = lax.shift_right_logical(n_pairs, 1)
    state = lax.fori_loop(0, n_quads, lambda qd, c: pair(2 * qd + 1, pair(2 * qd, c)), state)
    carry, _, acc = lax.fori_loop(2 * n_quads, n_pairs, pair, state)
    acc = accumulate(m - 1, e_scr[1], carry, acc)
    o_ref[...] = acc.T.astype(o_ref.dtype)


def _sb_attention(qkv, n_heads, tq=512, tk=256):
    b, s, _ = qkv.shape
    dm = n_heads * HEAD_DIM
    vt = jnp.swapaxes(qkv[:, :, 2 * dm:], 1, 2)
    pos = jnp.arange(tk)
    later_mat = (pos[None, :] > pos[:, None]).astype(BF16)
    return pl.pallas_call(
        functools.partial(_sb_attn_kernel, tq=tq, tk=tk),
        grid=(b, n_heads, s // tq),
        in_specs=[
            pl.BlockSpec((None, tq, HEAD_DIM), lambda b, h, i: (b, i, h)),
            pl.BlockSpec((None, s, HEAD_DIM), lambda b, h, i: (b, 0, n_heads + h)),
            pl.BlockSpec((None, HEAD_DIM, s), lambda b, h, i: (b, h, 0)),
            pl.BlockSpec((tk, tk), lambda b, h, i: (0, 0)),
        ],
        out_specs=pl.BlockSpec((None, tq, HEAD_DIM), lambda b, h, i: (b, i, h)),
        out_shape=jax.ShapeDtypeStruct((b, s, dm), BF16),
        scratch_shapes=[pltpu.VMEM((2, tk, tq), F32), pltpu.VMEM((2, tk, tq), F32)],
        compiler_params=_cparams(("parallel", "parallel", "parallel")),
        name="sb_attn",
    )(qkv, qkv, vt, later_mat)


def _s5_tables(lam_re, lam_im, log_dt, b_re, b_im, c_re, c_im):
    g, p, gc = b_re.shape
    l = S5_CHUNK
    gpt = LANES // gc
    nt = g // gpt
    hi = lax.Precision.HIGHEST
    dt = jnp.exp(log_dt.astype(F32))[:, None]
    lr = lam_re.astype(F32)
    li = lam_im.astype(F32)
    mag = jnp.exp(lr * dt)
    ar = mag * jnp.cos(li * dt)
    ai = mag * jnp.sin(li * dt)
    den = lr * lr + li * li
    fr = ((ar - 1.0) * lr + ai * li) / den
    fi = (ai * lr - (ar - 1.0) * li) / den
    br_ = b_re.astype(F32)
    bi_ = b_im.astype(F32)
    bbr = fr[..., None] * br_ - fi[..., None] * bi_
    bbi = fr[..., None] * bi_ + fi[..., None] * br_
    cr = c_re.astype(F32)
    ci = c_im.astype(F32)
    n = jnp.arange(l + 1, dtype=F32)[:, None, None]
    pw_r = jnp.exp(n * (lr * dt)) * jnp.cos(n * (li * dt))
    pw_i = jnp.exp(n * (lr * dt)) * jnp.sin(n * (li * dt))
    bt_r = bbr.transpose(0, 2, 1)
    bt_i = bbi.transpose(0, 2, 1)
    ab_r = pw_r[:l, :, None, :] * bt_r - pw_i[:l, :, None, :] * bt_i
    ab_i = pw_r[:l, :, None, :] * bt_i + pw_i[:l, :, None, :] * bt_r

    def tile_rows(t):
        cols = t.shape[-1]
        return t.reshape(l, nt, gpt, gc, cols).transpose(1, 0, 2, 3, 4).reshape(nt, l * LANES, cols)

    wz = tile_rows(jnp.concatenate([ab_r[::-1], ab_i[::-1]], axis=-1))

    kern = (jnp.einsum('gcp,ngdp->gdnc', cr, ab_r, precision=hi)
            - jnp.einsum('gcp,ngdp->gdnc', ci, ab_i, precision=hi)).reshape(g, gc, l * gc)
    intra = tile_rows(jnp.stack([jnp.pad(kern[:, :, :(l - j) * gc], ((0, 0), (0, 0), (j * gc, 0)))
                                 for j in range(l)]))

    ct_r = jnp.tile(cr.transpose(0, 2, 1), (1, 1, l))
    ct_i = jnp.tile(ci.transpose(0, 2, 1), (1, 1, l))
    p1_r = jnp.repeat(pw_r[1:].transpose(1, 2, 0), gc, axis=-1)
    p1_i = jnp.repeat(pw_i[1:].transpose(1, 2, 0), gc, axis=-1)
    vr = (ct_r * p1_r - ct_i * p1_i).reshape(nt, gpt * p, l * gc)
    vi = (-(ct_r * p1_i + ct_i * p1_r)).reshape(nt, gpt * p, l * gc)
    wy = jnp.concatenate([intra, vr, vi], axis=1)

    alr = pw_r[l].reshape(nt, 1, gpt * p)
    ali = pw_i[l].reshape(nt, 1, gpt * p)
    return wz.astype(BF16), wy.astype(BF16), alr, ali


def _spread_groups(src_ref, dst_ref, row_group_div, col_unit):
    n_src = src_ref.shape[1]
    n_dst = dst_ref.shape[1]
    groups = n_dst // n_src
    rc = 256
    sr = lax.broadcasted_iota(jnp.int32, (n_src, n_dst), 0)
    dc = lax.broadcasted_iota(jnp.int32, (n_src, n_dst), 1)
    spread = ((sr // col_unit == dc // (col_unit * groups)) & (sr % col_unit == dc % col_unit))
    spread = jnp.where(spread, 1.0, 0.0).astype(BF16)
    rr = lax.broadcasted_iota(jnp.int32, (rc, n_dst), 0)
    cg = (lax.broadcasted_iota(jnp.int32, (rc, n_dst), 1) // col_unit) % groups
    for r0 in range(0, src_ref.shape[0], rc):
        wide = jnp.dot(src_ref[r0:r0 + rc, :], spread, preferred_element_type=F32)
        keep = ((r0 + rr) // row_group_div) % groups == cg
        dst_ref[r0:r0 + rc, :] = jnp.where(keep, wide, 0.0).astype(dst_ref.dtype)


def _s5_core_kernel(ut_ref, wzc_ref, wyc_ref, alr_ref, ali_ref, d_ref, o_ref,
                    wz_ref, wy_ref, zr_ref, zi_ref, hr_ref, hi_ref, *, l, nk, gc, p):
    @pl.when(pl.program_id(1) == 0)
    def _():
        n_in = l * LANES
        _spread_groups(wzc_ref, wz_ref, gc, p)
        _spread_groups(wyc_ref.at[:n_in], wy_ref.at[:n_in], gc, gc)
        _spread_groups(wyc_ref.at[n_in:], wy_ref.at[n_in:], p, gc)

    us = [ut_ref[pl.ds(j, nk, stride=l), :] for j in range(l)]
    ub = jnp.concatenate([u.astype(BF16) for u in us], axis=1)
    z = jnp.dot(ub, wz_ref[...], preferred_element_type=F32)
    half = z.shape[1] // 2
    zr_ref[...] = z[:, :half]
    zi_ref[...] = z[:, half:]
    alr = alr_ref[...]
    ali = ali_ref[...]
    rows = lax.broadcasted_iota(jnp.int32, (SUBLANES, half), 0)

    def body(kb, c):
        hr, hi = c
        base = pl.multiple_of(kb * SUBLANES, SUBLANES)
        zr = zr_ref[pl.ds(base, SUBLANES), :]
        zi = zi_ref[pl.ds(base, SUBLANES), :]
        out_r = jnp.zeros((SUBLANES, half), F32)
        out_i = jnp.zeros((SUBLANES, half), F32)
        for r in range(SUBLANES):
            out_r = jnp.where(rows == r, hr, out_r)
            out_i = jnp.where(rows == r, hi, out_i)
            hr, hi = (alr * hr - ali * hi + zr[r:r + 1, :], alr * hi + ali * hr + zi[r:r + 1, :])
        hr_ref[pl.ds(base, SUBLANES), :] = out_r
        hi_ref[pl.ds(base, SUBLANES), :] = out_i
        return hr, hi

    zero = jnp.zeros((1, half), F32)
    lax.fori_loop(0, nk // SUBLANES, body, (zero, zero))

    lhs = jnp.concatenate([ub, hr_ref[...].astype(BF16), hi_ref[...].astype(BF16)], axis=1)
    y = jnp.dot(lhs, wy_ref[...], preferred_element_type=F32)
    d = d_ref[...]
    for i in range(l):
        yi = y[:, i * LANES:(i + 1) * LANES] + d * us[i]
        o_ref[pl.ds(i, nk, stride=l), :] = _gelu_tanh(yi).astype(o_ref.dtype)


def _s5_core(u, tables, d_skip):
    b, s, d = u.shape
    l = S5_CHUNK
    nk = s // l
    nt = d // LANES
    wz, wy, alr, ali = tables
    half = alr.shape[2]
    gpt = LANES // S5_GROUP
    return pl.pallas_call(
        functools.partial(_s5_core_kernel, l=l, nk=nk, gc=S5_GROUP, p=half // gpt),
        grid=(nt, b),
        in_specs=[
            pl.BlockSpec((None, s, LANES), lambda t, b: (b, 0, t)),
            pl.BlockSpec((None,) + wz.shape[1:], lambda t, b: (t, 0, 0)),
            pl.BlockSpec((None,) + wy.shape[1:], lambda t, b: (t, 0, 0)),
            pl.BlockSpec((None, 1, half), lambda t, b: (t, 0, 0)),
            pl.BlockSpec((None, 1, half), lambda t, b: (t, 0, 0)),
            pl.BlockSpec((None, 1, LANES), lambda t, b: (t, 0, 0)),
        ],
        out_specs=pl.BlockSpec((None, s, LANES), lambda t, b: (b, 0, t)),
        out_shape=jax.ShapeDtypeStruct((b, s, d), F32),
        scratch_shapes=[pltpu.VMEM((l * LANES, 2 * half), BF16),
                        pltpu.VMEM((l * LANES + 2 * half, l * LANES), BF16)] + [pltpu.VMEM((nk, half), F32)] * 4,
        compiler_params=_cparams(("arbitrary", "arbitrary")),
        name="s5_core",
    )(u, wz, wy, alr, ali, d_skip.astype(F32).reshape(nt, 1, LANES))


def _rope_tab_kernel(pos_ref, f_head_ref, f_idx_ref, ch_ref, sh_ref, ci_ref, si_ref):
    p = pos_ref[...].astype(F32)
    lane = lax.broadcasted_iota(jnp.int32, ch_ref.shape, 1)
    a = p * f_head_ref[...]
    ch_ref[...] = jnp.cos(a)
    sh_ref[...] = jnp.where(lane < HEAD_DIM // 2, -jnp.sin(a), jnp.sin(a))
    a = p * f_idx_ref[...]
    ci_ref[...] = jnp.cos(a)
    si_ref[...] = jnp.where(lane % IDX_DIM < IDX_DIM // 2, -jnp.sin(a), jnp.sin(a))


def _rope_tables(positions, tm=1024):
    b, s = positions.shape

    def inv_freq(dim):
        return ROPE_THETA ** (-jnp.arange(0, dim, 2, dtype=F32) / dim)

    f_head = jnp.tile(inv_freq(HEAD_DIM), 2).reshape(1, LANES)
    f_idx = jnp.tile(inv_freq(IDX_DIM), 2 * LANES // IDX_DIM).reshape(1, LANES)
    row_spec = pl.BlockSpec((None, tm, LANES), lambda b, i: (b, i, 0))
    return pl.pallas_call(
        _rope_tab_kernel,
        grid=(b, s // tm),
        in_specs=[
            pl.BlockSpec((None, tm, 1), lambda b, i: (b, i, 0)),
            pl.BlockSpec((1, LANES), lambda b, i: (0, 0)),
            pl.BlockSpec((1, LANES), lambda b, i: (0, 0)),
        ],
        out_specs=[row_spec] * 4,
        out_shape=[jax.ShapeDtypeStruct((b, s, LANES), F32)] * 4,
        compiler_params=_cparams(("parallel", "parallel")),
        name="rope_tables",
    )(positions.reshape(b, s, 1), f_head, f_idx)


def _rope_head(y, cos, sin_signed):
    return y * cos + pltpu.roll(y, HEAD_DIM // 2, axis=1) * sin_signed


def _rope_idx(y, cos, sin_signed):
    lane = lax.broadcasted_iota(jnp.int32, y.shape, 1)
    half = IDX_DIM // 2
    partner = jnp.where(lane % IDX_DIM < half, pltpu.roll(y, LANES - half, axis=1), pltpu.roll(y, half, axis=1))
    return y * cos + partner * sin_signed


def _dsa_inproj_kernel(x_ref, g_ref, sh_ref, sc_ref, w_ref, gain_ref, ch_ref, shd_ref, ci_ref, si_ref,
                       o_ref, tail_ref, h_ref, *, n_qk, n_v, n_qi):
    j = pl.program_id(2)

    @pl.when(j == 0)
    def _():
        h_ref[...] = _rms_mod(x_ref, g_ref, sh_ref, sc_ref).astype(BF16)

    y = jnp.dot(h_ref[...], w_ref[...], preferred_element_type=F32)
    slices = [slice(hh * LANES, (hh + 1) * LANES) for hh in range(y.shape[1] // LANES)]

    @pl.when(j < n_qk)
    def _():
        for sl in slices:
            yn = _head_rmsnorm(y[:, sl], gain_ref[:, sl])
            o_ref[:, sl] = _rope_head(yn, ch_ref[...], shd_ref[...]).astype(o_ref.dtype)

    @pl.when((j >= n_qk) & (j < n_qk + n_v))
    def _():
        o_ref[...] = y.astype(o_ref.dtype)

    @pl.when((j >= n_qk + n_v) & (j < n_qk + n_v + n_qi))
    def _():
        for sl in slices:
            o_ref[:, sl] = (_rope_idx(y[:, sl], ci_ref[...], si_ref[...]) * IDX_DIM ** -0.5).astype(o_ref.dtype)

    @pl.when(j == n_qk + n_v + n_qi)
    def _():
        t = y[:, :LANES]
        lane = lax.broadcasted_iota(jnp.int32, t.shape, 1)
        tail_ref[...] = jnp.where(lane < IDX_DIM, _rope_idx(t, ci_ref[...], si_ref[...]), t * IDX_HEADS ** -0.5)


def _dsa_inproj(x, ln_g, mod3, w_pad, gain_row, tabs, n_heads, tm=1024, tn=512):
    b, s, d = x.shape
    dm = n_heads * HEAD_DIM
    n_qk, n_v, n_qi = 2 * dm // tn, dm // tn, IDX_HEADS * IDX_DIM // tn
    n_main = n_qk + n_v + n_qi
    tab_spec = pl.BlockSpec((None, tm, LANES), lambda b, i, j: (b, i, 0))
    return pl.pallas_call(
        functools.partial(_dsa_inproj_kernel, n_qk=n_qk, n_v=n_v, n_qi=n_qi),
        grid=(b, s // tm, n_main + 1),
        in_specs=[
            pl.BlockSpec((None, tm, d), lambda b, i, j: (b, i, 0)),
            pl.BlockSpec((1, d), lambda b, i, j: (0, 0)),
            *_mod_specs(d, 0, 1),
            pl.BlockSpec((d, tn), lambda b, i, j: (0, j)),
            pl.BlockSpec((1, tn), lambda b, i, j: (0, j)),
            tab_spec, tab_spec, tab_spec, tab_spec,
        ],
        out_specs=[
            pl.BlockSpec((None, tm, tn), lambda b, i, j: (b, i, jnp.minimum(j, n_main - 1))),
            pl.BlockSpec((None, tm, LANES), lambda b, i, j: (b, i, 0)),
        ],
        out_shape=[jax.ShapeDtypeStruct((b, s, n_main * tn), BF16),
                   jax.ShapeDtypeStruct((b, s, LANES), F32)],
        scratch_shapes=[pltpu.VMEM((tm, d), BF16)],
        compiler_params=_cparams(("parallel", "parallel", "arbitrary")),
        name="dsa_inproj",
    )(x, ln_g.reshape(1, d), mod3, mod3, w_pad, gain_row, *tabs)


def _dsa_core_kernel(q_ref, k_ref, vt_ref, qi_ref, ki_ref, wi_ref, o_ref, key_ref, bias_ref, s_scr, p_scr,
                     *, t, topk, idx_bits, sub):
    i = pl.program_id(1)
    h = pl.program_id(2)
    nk = i + 1
    kf = float(topk)

    def chunk(c):
        return pl.ds(pl.multiple_of(c * t, t), t)

    @pl.when(h == 0)
    def _():
        qi = qi_ref[...]
        wi_t = wi_ref[...].T
        row = lax.broadcasted_iota(jnp.int32, (t, t), 0)
        col = lax.broadcasted_iota(jnp.int32, (t, t), 1)

        def score_body(c, _):
            ki = ki_ref[chunk(c), :][:, :IDX_DIM].astype(BF16)
            score = jnp.zeros((t, t), F32)
            for hh in range(IDX_HEADS):
                rel = lax.dot_general(ki, qi[:, hh * IDX_DIM:(hh + 1) * IDX_DIM], (((1,), (1,)), ((), ())),
                                      preferred_element_type=F32)
                score = score + wi_t[IDX_DIM + hh:IDX_DIM + hh + 1, :] * jnp.maximum(rel, 0.0)
            bits = pltpu.bitcast(score + 0.0, jnp.int32)
            key = jnp.where(bits < 0, bits ^ jnp.int32(0x7FFFFFFF), bits)
            key = jnp.where((c - i) * t + row <= col, key, jnp.int32(INT_MIN))
            key_ref[chunk(c), :] = key
            return 0

        lax.fori_loop(0, nk, score_body, 0)

        sub_row = lax.broadcasted_iota(jnp.int32, (sub, t), 0)

        def count(pred, *query_args):
            def body(c, acc):
                for r in range(t // sub):
                    start = pl.multiple_of(c * t + r * sub, sub)
                    hit = pred(key_ref[pl.ds(start, sub), :], start + sub_row, *query_args)
                    ones = jnp.where(hit, 1.0, 0.0)
                    acc = acc + jnp.sum(ones.reshape(sub // SUBLANES, SUBLANES, t), axis=0)
                return acc
            acc = lax.fori_loop(0, nk, body, jnp.zeros((SUBLANES, t), F32))
            return jnp.sum(acc, axis=0, keepdims=True)

        def bisect(n, thr):
            cand = thr + lax.shift_left(jnp.int32(1), 31 - n)
            return jnp.where(count(lambda key, pos, cd: key >= cd, cand) >= kf, cand, thr)

        thr = lax.fori_loop(0, 32, bisect, jnp.full((1, t), INT_MIN, jnp.int32))
        thr = jnp.maximum(thr, jnp.int32(INT_MIN + 1))
        n_ge = count(lambda key, pos, th: key >= th, thr)

        def tie_cut(_):
            need = kf - count(lambda key, pos, th: key > th, thr)
            def step(n, cut):
                cand = cut + lax.shift_left(jnp.int32(1), idx_bits - 1 - n)
                below = count(lambda key, pos, th, cd: (key == th) & (pos < cd), thr, cand)
                return jnp.where(below < need, cand, cut)
            return lax.fori_loop(0, idx_bits, step, jnp.zeros((1, t), jnp.int32))

        cut = lax.cond(jnp.max(n_ge) > kf, tie_cut, lambda _: jnp.full((1, t), 2 ** idx_bits - 1, jnp.int32), 0)

        def bias_body(c, _):
            key = key_ref[chunk(c), :]
            sel = (key > thr) | ((key == thr) & (c * t + row <= cut))
            bias_ref[chunk(c), :] = jnp.where(sel, 0.0, NEG_BIG)
            return 0

        lax.fori_loop(0, nk, bias_body, 0)

    q = q_ref[...]

    def logits(c):
        kc = k_ref[chunk(jnp.minimum(c, nk - 1)), :]
        return lax.dot_general(kc, q, (((1,), (1,)), ((), ())), preferred_element_type=F32)

    def weights(c, s, m, l):
        s = s + bias_ref[chunk(c), :]
        m_new = jnp.maximum(m, jnp.max(s, axis=0, keepdims=True))
        alpha = jnp.exp2(m - m_new)
        p = jnp.exp2(s - m_new)
        return p.astype(BF16), alpha, m_new, alpha * l + jnp.sum(p, axis=0, keepdims=True)

    def values(c, p, alpha, acc):
        return alpha * acc + jnp.dot(vt_ref[:, chunk(c)], p, preferred_element_type=F32)

    def step(n, st, slot):
        alpha, m, l, acc = st
        s_scr[1 - slot] = logits(n + 2)
        acc = values(n, p_scr[slot], alpha, acc)
        p, alpha, m, l = weights(n + 1, s_scr[slot], m, l)
        p_scr[1 - slot] = p
        return alpha, m, l, acc

    p0, alpha0, m0, l0 = weights(0, logits(0), jnp.full((1, t), NEG_BIG, F32), jnp.zeros((1, t), F32))
    p_scr[0] = p0
    s_scr[0] = logits(1)
    n_steps = nk - 1
    st = (alpha0, m0, l0, jnp.zeros((HEAD_DIM, t), F32))
    st = lax.fori_loop(0, lax.shift_right_logical(n_steps, 1),
                       lambda pp, c: step(2 * pp + 1, step(2 * pp, c, 0), 1), st)
    alpha, _, l, acc = lax.cond((n_steps & 1) == 1, lambda c: step(n_steps - 1, c, 0), lambda c: c, st)
    acc = values(nk - 1, p_scr[n_steps & 1], alpha, acc)
    o_ref[...] = (acc / l).T.astype(o_ref.dtype)


def _dsa_core(main, tail, n_heads, t=512):
    b, s, _ = main.shape
    dm = n_heads * HEAD_DIM
    topk = min(DSA_TOPK_MAX, s // 4)
    idx_bits = max(1, (s - 1).bit_length())
    qi_block = (3 * dm) // (IDX_HEADS * IDX_DIM)
    vt = jnp.swapaxes(main[:, :, 2 * dm:3 * dm], 1, 2)
    return pl.pallas_call(
        functools.partial(_dsa_core_kernel, t=t, topk=topk, idx_bits=idx_bits, sub=min(t, 64)),
        grid=(b, s // t, n_heads),
        in_specs=[
            pl.BlockSpec((None, t, HEAD_DIM), lambda b, i, h: (b, i, h)),
            pl.BlockSpec((None, s, HEAD_DIM), lambda b, i, h: (b, 0, n_heads + h)),
            pl.BlockSpec((None, HEAD_DIM, s), lambda b, i, h: (b, h, 0)),
            pl.BlockSpec((None, t, IDX_HEADS * IDX_DIM), lambda b, i, h: (b, i, qi_block)),
            pl.BlockSpec((None, s, LANES), lambda b, i, h: (b, 0, 0)),
            pl.BlockSpec((None, t, LANES), lambda b, i, h: (b, i, 0)),
        ],
        out_specs=pl.BlockSpec((None, t, HEAD_DIM), lambda b, i, h: (b, i, h)),
        out_shape=jax.ShapeDtypeStruct((b, s, dm), BF16),
        scratch_shapes=[pltpu.VMEM((s, t), jnp.int32), pltpu.VMEM((s, t), F32),
                        pltpu.VMEM((2, t, t), F32), pltpu.VMEM((2, t, t), BF16)],
        compiler_params=_cparams(("parallel", "parallel", "arbitrary")),
        name="dsa_core",
    )(main, main, vt, main, tail, tail)


def kernel(x, c, positions, ln1_g, ln2_g, ada_w, ada_b, mlp_w1, mlp_w2, sb_w_in, sb_q_gain, sb_k_gain, sb_w_out, s5_w_in, s5_lambda_re, s5_lambda_im, s5_log_dt, s5_b_re, s5_b_im, s5_c_re, s5_c_im, s5_d, s5_w_glu, dsa_w_in, dsa_q_gain, dsa_k_gain, dsa_w_out):
    depth = ada_w.shape[0]
    b, s, d = x.shape
    n_heads = d // HEAD_DIM
    scale = HEAD_DIM ** -0.5

    mod = _ada_mod(c, ada_w, ada_b)
    rope_tabs = _rope_tables(positions) if depth > 2 else None
    mlp_w1, mlp_w2, sb_w_in, sb_w_out, s5_w_in, s5_w_glu, dsa_w_out = (
        w.astype(BF16) for w in (mlp_w1, mlp_w2, sb_w_in, sb_w_out, s5_w_in, s5_w_glu, dsa_w_out))

    counts = [0, 0, 0]
    for i in range(depth):
        mod3 = mod[i].reshape(SUBLANES, 1, 6 * d)
        kind = i % N_MIXERS
        j = counts[kind]
        counts[kind] += 1
        if kind == 0:
            gain = jnp.concatenate([jnp.tile(sb_q_gain[j].astype(F32) * (scale * LOG2_E), n_heads),
                                    jnp.tile(sb_k_gain[j].astype(F32), n_heads),
                                    jnp.ones((d,), F32)]).reshape(1, 3 * d)
            qkv = _sb_inproj(x, ln1_g[i], mod3, sb_w_in, j, gain)
            o = _sb_attention(qkv, n_heads)
            x = _out_gate(o, sb_w_out, j, x, mod3, 2)
        elif kind == 1:
            u = _nm_matmul(x, ln1_g[i], mod3, s5_w_in, j, F32)
            tables = _s5_tables(s5_lambda_re[j], s5_lambda_im[j], s5_log_dt[j], s5_b_re[j], s5_b_im[j],
                                s5_c_re[j], s5_c_im[j])
            z = _s5_core(u, tables, s5_d[j])
            x = _glu_gate(z, s5_w_glu, j, x, mod3, 2)
        else:
            tn = 512
            n_in = dsa_w_in.shape[2]
            n_main = 3 * d + IDX_HEADS * IDX_DIM
            w_pad = jnp.zeros((d, n_main + tn), BF16).at[:, :n_in].set(dsa_w_in[j].astype(BF16))
            gain = jnp.zeros((1, n_main + tn), F32)
            gain = gain.at[0, :d].set(jnp.tile(dsa_q_gain[j].astype(F32) * (scale * LOG2_E), n_heads))
            gain = gain.at[0, d:2 * d].set(jnp.tile(dsa_k_gain[j].astype(F32), n_heads))
            main, tail = _dsa_inproj(x, ln1_g[i], mod3, w_pad, gain, rope_tabs, n_heads, tn=tn)
            o = _dsa_core(main, tail, n_heads)
            x = _out_gate(o, dsa_w_out, j, x, mod3, 2)
        x = _mlp(x, ln2_g[i], mod3, mlp_w1, mlp_w2, i)
    return x
```

```python
import functools
import math

import jax
import jax.numpy as jnp
from jax import lax
from jax.experimental import pallas as pl
from jax.experimental.pallas import tpu as pltpu

F32 = jnp.float32
BF16 = jnp.bfloat16

N_MIXERS = 3
HEAD_DIM = 128
ROPE_THETA = 10000.0
EPS = 1e-6
S5_GROUP = 16
S5_STATE = 64
IDX_HEADS = 16
IDX_DIM = 64
DSA_TOPK_MAX = 256

LANES = 128
SUBLANES = 8
ROW_STRIP = 16
VMEM_LIMIT_BYTES = 56 * 1024 * 1024

S5_CHUNK = 16
NEG_BIG = -1e30
INT_MIN = -(2 ** 31)
LOG2_E = 1.4426950408889634


def _cparams(sem):
    return pltpu.CompilerParams(dimension_semantics=sem, vmem_limit_bytes=VMEM_LIMIT_BYTES)


def _sigmoid(x):
    return 1.0 / (1.0 + jnp.exp(-x))


def _gelu_tanh(x):
    c = math.sqrt(2.0 / math.pi)
    return 0.5 * x * (1.0 + jnp.tanh(c * (x + 0.044715 * (x * x * x))))


def _rms_mod_store(h_ref, x_ref, g_ref, sh_ref, sc_ref):
    d = x_ref.shape[1]
    gain = jnp.broadcast_to(g_ref[...] * (1.0 + sc_ref[...]), (ROW_STRIP, d))
    shift = jnp.broadcast_to(sh_ref[...], (ROW_STRIP, d))

    def body(r, _):
        rows = pl.ds(pl.multiple_of(r * ROW_STRIP, ROW_STRIP), ROW_STRIP)
        x = x_ref[rows, :]
        ms = jnp.mean(x * x, axis=-1, keepdims=True)
        h_ref[rows, :] = (x * lax.rsqrt(ms + EPS) * gain + shift).astype(h_ref.dtype)
        return 0

    lax.fori_loop(0, x_ref.shape[0] // ROW_STRIP, body, 0, unroll=8)


def _head_rmsnorm(y, gain):
    ms = jnp.mean(y * y, axis=-1, keepdims=True)
    return y * lax.rsqrt(ms + EPS) * gain


def _ada_kernel(c_ref, w_ref, b_ref, o_ref):
    c = c_ref[...]
    cond = (c * _sigmoid(c)).astype(BF16)
    o_ref[...] = jnp.dot(cond, w_ref[...].astype(BF16), preferred_element_type=F32) + b_ref[...]


def _ada_mod(c, ada_w, ada_b):
    depth, d, n = ada_w.shape
    b = c.shape[0]
    c8 = jnp.zeros((SUBLANES, d), F32).at[:b].set(c)
    tn = 1024
    return pl.pallas_call(
        _ada_kernel,
        grid=(depth, n // tn),
        in_specs=[
            pl.BlockSpec((SUBLANES, d), lambda l, j: (0, 0)),
            pl.BlockSpec((None, d, tn), lambda l, j: (l, 0, j)),
            pl.BlockSpec((None, 1, tn), lambda l, j: (l, 0, j)),
        ],
        out_specs=pl.BlockSpec((None, SUBLANES, tn), lambda l, j: (l, 0, j)),
        out_shape=jax.ShapeDtypeStruct((depth, SUBLANES, n), F32),
        compiler_params=_cparams(("parallel", "parallel")),
        name="ada_mod",
    )(c8, ada_w, ada_b.reshape(depth, 1, n))


def _mod_specs(d, shift_chunk, scale_chunk):
    return [
        pl.BlockSpec((None, 1, d), lambda b, i, j: (b, 0, shift_chunk)),
        pl.BlockSpec((None, 1, d), lambda b, i, j: (b, 0, scale_chunk)),
    ]


def _nm_matmul_kernel(x_ref, g_ref, sh_ref, sc_ref, w_ref, o_ref, h_ref):
    @pl.when(pl.program_id(2) == 0)
    def _():
        _rms_mod_store(h_ref, x_ref, g_ref, sh_ref, sc_ref)

    o_ref[...] = jnp.dot(h_ref[...], w_ref[...], preferred_element_type=F32).astype(o_ref.dtype)


def _nm_matmul(x, ln_g, mod3, w, layer, out_dtype, tm=1024, tn=1024):
    b, s, d = x.shape
    n = w.shape[2]
    return pl.pallas_call(
        _nm_matmul_kernel,
        grid=(b, s // tm, n // tn),
        in_specs=[
            pl.BlockSpec((None, tm, d), lambda b, i, j: (b, i, 0)),
            pl.BlockSpec((1, d), lambda b, i, j: (0, 0)),
            *_mod_specs(d, 0, 1),
            pl.BlockSpec((None, d, tn), lambda b, i, j: (layer, 0, j)),
        ],
        out_specs=pl.BlockSpec((None, tm, tn), lambda b, i, j: (b, i, j)),
        out_shape=jax.ShapeDtypeStruct((b, s, n), out_dtype),
        scratch_shapes=[pltpu.VMEM((tm, d), BF16)],
        compiler_params=_cparams(("parallel", "parallel", "arbitrary")),
        name="nm_matmul",
    )(x, ln_g.reshape(1, d), mod3, mod3, w)


def _sb_inproj_kernel(x_ref, g_ref, sh_ref, sc_ref, w_ref, gain_ref, o_ref, h_ref, *, n_norm_tiles):
    j = pl.program_id(2)

    @pl.when(j == 0)
    def _():
        _rms_mod_store(h_ref, x_ref, g_ref, sh_ref, sc_ref)

    y = jnp.dot(h_ref[...], w_ref[...], preferred_element_type=F32)

    @pl.when(j < n_norm_tiles)
    def _():
        for hh in range(y.shape[1] // HEAD_DIM):
            sl = slice(hh * HEAD_DIM, (hh + 1) * HEAD_DIM)
            o_ref[:, sl] = _head_rmsnorm(y[:, sl], gain_ref[:, sl]).astype(o_ref.dtype)

    @pl.when(j >= n_norm_tiles)
    def _():
        o_ref[...] = y.astype(o_ref.dtype)


def _sb_inproj(x, ln_g, mod3, w, layer, gain_row, tm=1024, tn=1024):
    b, s, d = x.shape
    n = w.shape[2]
    return pl.pallas_call(
        functools.partial(_sb_inproj_kernel, n_norm_tiles=2 * d // tn),
        grid=(b, s // tm, n // tn),
        in_specs=[
            pl.BlockSpec((None, tm, d), lambda b, i, j: (b, i, 0)),
            pl.BlockSpec((1, d), lambda b, i, j: (0, 0)),
            *_mod_specs(d, 0, 1),
            pl.BlockSpec((None, d, tn), lambda b, i, j: (layer, 0, j)),
            pl.BlockSpec((1, tn), lambda b, i, j: (0, j)),
        ],
        out_specs=pl.BlockSpec((None, tm, tn), lambda b, i, j: (b, i, j)),
        out_shape=jax.ShapeDtypeStruct((b, s, n), BF16),
        scratch_shapes=[pltpu.VMEM((tm, d), BF16)],
        compiler_params=_cparams(("parallel", "parallel", "arbitrary")),
        name="sb_inproj",
    )(x, ln_g.reshape(1, d), mod3, mod3, w, gain_row)


def _out_gate_kernel(a_ref, w_ref, x_ref, gate_ref, o_ref):
    y = jnp.dot(a_ref[...], w_ref[...], preferred_element_type=F32)
    o_ref[...] = x_ref[...] + gate_ref[...] * y


def _out_gate(a, w, layer, x, mod3, gate_chunk, tm=1024, tn=1024):
    b, s, d = x.shape
    k = a.shape[2]
    nt = d // tn
    return pl.pallas_call(
        _out_gate_kernel,
        grid=(b, s // tm, nt),
        in_specs=[
            pl.BlockSpec((None, tm, k), lambda b, i, j: (b, i, 0)),
            pl.BlockSpec((None, k, tn), lambda b, i, j: (layer, 0, j)),
            pl.BlockSpec((None, tm, tn), lambda b, i, j: (b, i, j)),
            pl.BlockSpec((None, 1, tn), lambda b, i, j: (b, 0, gate_chunk * nt + j)),
        ],
        out_specs=pl.BlockSpec((None, tm, tn), lambda b, i, j: (b, i, j)),
        out_shape=jax.ShapeDtypeStruct((b, s, d), F32),
        compiler_params=_cparams(("parallel", "parallel", "parallel")),
        name="out_gate",
    )(a, w, x, mod3)


def _glu_gate_kernel(z_ref, wa_ref, wg_ref, x_ref, gate_ref, o_ref, zb_ref):
    @pl.when(pl.program_id(2) == 0)
    def _():
        zb_ref[...] = z_ref[...].astype(BF16)

    z = zb_ref[...]
    a = jnp.dot(z, wa_ref[...], preferred_element_type=F32)
    g = jnp.dot(z, wg_ref[...], preferred_element_type=F32)
    o_ref[...] = x_ref[...] + gate_ref[...] * (a * _sigmoid(g))


def _glu_gate(z, w_glu, layer, x, mod3, gate_chunk, tm=1024, tn=512):
    b, s, d = x.shape
    nt = d // tn
    return pl.pallas_call(
        _glu_gate_kernel,
        grid=(b, s // tm, nt),
        in_specs=[
            pl.BlockSpec((None, tm, d), lambda b, i, j: (b, i, 0)),
            pl.BlockSpec((None, d, tn), lambda b, i, j: (layer, 0, j)),
            pl.BlockSpec((None, d, tn), lambda b, i, j: (layer, 0, nt + j)),
            pl.BlockSpec((None, tm, tn), lambda b, i, j: (b, i, j)),
            pl.BlockSpec((None, 1, tn), lambda b, i, j: (b, 0, gate_chunk * nt + j)),
        ],
        out_specs=pl.BlockSpec((None, tm, tn), lambda b, i, j: (b, i, j)),
        out_shape=jax.ShapeDtypeStruct((b, s, d), F32),
        scratch_shapes=[pltpu.VMEM((tm, d), BF16)],
        compiler_params=_cparams(("parallel", "parallel", "arbitrary")),
        name="glu_gate",
    )(z, w_glu, w_glu, x, mod3)


def _mlp_kernel(x_ref, g_ref, sh_ref, sc_ref, w1_ref, w2_ref, gate_ref, o_ref, h_ref, acc_ref):
    j = pl.program_id(2)

    @pl.when(j == 0)
    def _():
        _rms_mod_store(h_ref, x_ref, g_ref, sh_ref, sc_ref)
        acc_ref[...] = jnp.zeros_like(acc_ref)

    a = jnp.maximum(jnp.dot(h_ref[...], w1_ref[...], preferred_element_type=F32), 0.0)
    acc_ref[...] += jnp.dot((a * a).astype(BF16), w2_ref[...], preferred_element_type=F32)

    @pl.when(j == pl.num_programs(2) - 1)
    def _():
        o_ref[...] = x_ref[...] + gate_ref[...] * acc_ref[...]


def _mlp(x, ln_g, mod3, w1, w2, layer, tm=512, tf=1024):
    b, s, d = x.shape
    f = w1.shape[2]
    return pl.pallas_call(
        _mlp_kernel,
        grid=(b, s // tm, f // tf),
        in_specs=[
            pl.BlockSpec((None, tm, d), lambda b, i, j: (b, i, 0)),
            pl.BlockSpec((1, d), lambda b, i, j: (0, 0)),
            *_mod_specs(d, 3, 4),
            pl.BlockSpec((None, d, tf), lambda b, i, j: (layer, 0, j)),
            pl.BlockSpec((None, tf, d), lambda b, i, j: (layer, j, 0)),
            pl.BlockSpec((None, 1, d), lambda b, i, j: (b, 0, 5)),
        ],
        out_specs=pl.BlockSpec((None, tm, d), lambda b, i, j: (b, i, 0)),
        out_shape=jax.ShapeDtypeStruct((b, s, d), F32),
        scratch_shapes=[pltpu.VMEM((tm, d), BF16), pltpu.VMEM((tm, d), F32)],
        compiler_params=_cparams(("parallel", "parallel", "arbitrary")),
        name="mlp",
    )(x, ln_g.reshape(1, d), mod3, mod3, w1, w2, mod3)


def _sb_attn_kernel(q_ref, k_ref, v_ref, later_ref, o_ref, z_scr, e_scr, *, tq, tk):
    i = pl.program_id(2)
    m = (i + 1) * (tq // tk)
    n_diag = tq // tk
    q = q_ref[...]
    lead = lax.broadcasted_iota(jnp.int32, (tq, tk), 0) - lax.broadcasted_iota(jnp.int32, (tq, tk), 1)
    sign = jnp.uint32(0x80000000)

    def key_start(n):
        return pl.multiple_of(jnp.maximum(m - 1 - n, 0) * tk, tk)

    def score(n):
        kb = k_ref[pl.ds(key_start(n), tk), :]
        return lax.dot_general(q, kb, (((1,), (1,)), ((), ())), preferred_element_type=F32)

    def log_weights(n, z, masked):
        neg_abs = pltpu.bitcast(pltpu.bitcast(z, jnp.uint32) | sign, F32)
        ls = jnp.minimum(z, 0.0) - jnp.log(1.0 + jnp.exp2(neg_abs)) * LOG2_E
        ln = ls - z
        if masked:
            causal = key_start(n) - i * tq < lead
            ln = jnp.where(causal, ln, 0.0)
            ls = jnp.where(causal, ls, NEG_BIG)
        ln = ln.astype(BF16)
        later = jnp.dot(ln, later_ref[...], preferred_element_type=F32)
        return ls + later, later[:, 0:1] + ln[:, 0:1].astype(F32)

    def accumulate(n, e, carry, acc):
        vb = v_ref[pl.ds(key_start(n), tk), :]
        w = jnp.exp2(e + carry).astype(BF16)
        return acc + jnp.dot(w, vb, preferred_element_type=F32)

    e0, tot0 = log_weights(0, score(0), True)
    e_scr[0] = e0
    z_scr[0] = score(1)

    def step(n, c):
        carry, tot, acc = c
        slot = n % 2
        z_scr[1 - slot] = score(n + 2)
        acc = accumulate(n, e_scr[slot], carry, acc)
        e_next, tot_next = log_weights(n + 1, z_scr[slot], True)
        e_scr[1 - slot] = e_next
        return carry + tot, tot_next, acc

    def pair(p, c):
        n = n_diag - 1 + 2 * p
        carry, tot, acc = c
        z_mid = score(n + 2)
        acc = accumulate(n, e_scr[1], carry, acc)
        e_mid, tot_mid = log_weights(n + 1, z_scr[1], False)
        carry = carry + tot
        z_scr[1] = score(n + 3)
        acc = accumulate(n + 1, e_mid, carry, acc)
        e_last, tot_last = log_weights(n + 2, z_mid, False)
        e_scr[1] = e_last
        return carry + tot_mid, tot_last, acc

    state = (jnp.zeros((tq, 1), F32), tot0, jnp.zeros((tq, HEAD_DIM), F32))
    assert n_diag % 2 == 0
    for n in range(n_diag - 1):
        state = step(n, state)
    carry, _, acc = lax.fori_loop(0, (m - n_diag) // 2, pair, state)
    acc = accumulate(m - 1, e_scr[1], carry, acc)
    o_ref[...] = acc.astype(o_ref.dtype)


def _sb_attention(qkv, n_heads, tq=512, tk=256):
    b, s, _ = qkv.shape
    pos = jnp.arange(tk)
    later_mat = (pos[:, None] > pos[None, :]).astype(BF16)
    return pl.pallas_call(
        functools.partial(_sb_attn_kernel, tq=tq, tk=tk),
        grid=(b, n_heads, s // tq),
        in_specs=[
            pl.BlockSpec((None, tq, HEAD_DIM), lambda b, h, i: (b, i, h)),
            pl.BlockSpec((None, s, HEAD_DIM), lambda b, h, i: (b, 0, n_heads + h)),
            pl.BlockSpec((None, s, HEAD_DIM), lambda b, h, i: (b, 0, 2 * n_heads + h)),
            pl.BlockSpec((tk, tk), lambda b, h, i: (0, 0)),
        ],
        out_specs=pl.BlockSpec((None, tq, HEAD_DIM), lambda b, h, i: (b, i, h)),
        out_shape=jax.ShapeDtypeStruct((b, s, n_heads * HEAD_DIM), BF16),
        scratch_shapes=[pltpu.VMEM((2, tq, tk), F32), pltpu.VMEM((2, tq, tk), F32)],
        compiler_params=_cparams(("parallel", "parallel", "parallel")),
        name="sb_attn",
    )(qkv, qkv, qkv, later_mat)


def _s5_tables(lam_re, lam_im, log_dt, b_re, b_im, c_re, c_im):
    g, p, gc = b_re.shape
    l = S5_CHUNK
    gpt = LANES // gc
    nt = g // gpt
    hi = lax.Precision.HIGHEST
    dt = jnp.exp(log_dt.astype(F32))[:, None]
    lr = lam_re.astype(F32)
    li = lam_im.astype(F32)
    mag = jnp.exp(lr * dt)
    ar = mag * jnp.cos(li * dt)
    ai = mag * jnp.sin(li * dt)
    den = lr * lr + li * li
    fr = ((ar - 1.0) * lr + ai * li) / den
    fi = (ai * lr - (ar - 1.0) * li) / den
    br_ = b_re.astype(F32)
    bi_ = b_im.astype(F32)
    bbr = fr[..., None] * br_ - fi[..., None] * bi_
    bbi = fr[..., None] * bi_ + fi[..., None] * br_
    cr = c_re.astype(F32)
    ci = c_im.astype(F32)
    n = jnp.arange(l + 1, dtype=F32)[:, None, None]
    pw_r = jnp.exp(n * (lr * dt)) * jnp.cos(n * (li * dt))
    pw_i = jnp.exp(n * (lr * dt)) * jnp.sin(n * (li * dt))
    bt_r = bbr.transpose(0, 2, 1)
    bt_i = bbi.transpose(0, 2, 1)
    ab_r = pw_r[:l, :, None, :] * bt_r - pw_i[:l, :, None, :] * bt_i
    ab_i = pw_r[:l, :, None, :] * bt_i + pw_i[:l, :, None, :] * bt_r

    def tile_rows(t):
        cols = t.shape[-1]
        return t.reshape(l, nt, gpt, gc, cols).transpose(1, 0, 2, 3, 4).reshape(nt, l * LANES, cols)

    wz = tile_rows(jnp.concatenate([ab_r[::-1], ab_i[::-1]], axis=-1))

    kern = (jnp.einsum('gcp,ngdp->gdnc', cr, ab_r, precision=hi)
            - jnp.einsum('gcp,ngdp->gdnc', ci, ab_i, precision=hi)).reshape(g, gc, l * gc)
    intra = tile_rows(jnp.stack([jnp.pad(kern[:, :, :(l - j) * gc], ((0, 0), (0, 0), (j * gc, 0)))
                                 for j in range(l)]))

    ct_r = jnp.tile(cr.transpose(0, 2, 1), (1, 1, l))
    ct_i = jnp.tile(ci.transpose(0, 2, 1), (1, 1, l))
    p1_r = jnp.repeat(pw_r[1:].transpose(1, 2, 0), gc, axis=-1)
    p1_i = jnp.repeat(pw_i[1:].transpose(1, 2, 0), gc, axis=-1)
    vr = (ct_r * p1_r - ct_i * p1_i).reshape(nt, gpt * p, l * gc)
    vi = (-(ct_r * p1_i + ct_i * p1_r)).reshape(nt, gpt * p, l * gc)
    wy = jnp.concatenate([intra, vr, vi], axis=1)

    alr = pw_r[l].reshape(nt, 1, gpt * p)
    ali = pw_i[l].reshape(nt, 1, gpt * p)
    return wz.astype(BF16), wy.astype(BF16), alr, ali


def _spread_groups(src_ref, dst_ref, row_group_div, col_unit):
    n_src = src_ref.shape[1]
    n_dst = dst_ref.shape[1]
    groups = n_dst // n_src
    rc = 256
    sr = lax.broadcasted_iota(jnp.int32, (n_src, n_dst), 0)
    dc = lax.broadcasted_iota(jnp.int32, (n_src, n_dst), 1)
    spread = ((sr // col_unit == dc // (col_unit * groups)) & (sr % col_unit == dc % col_unit))
    spread = jnp.where(spread, 1.0, 0.0).astype(BF16)
    rr = lax.broadcasted_iota(jnp.int32, (rc, n_dst), 0)
    cg = (lax.broadcasted_iota(jnp.int32, (rc, n_dst), 1) // col_unit) % groups
    for r0 in range(0, src_ref.shape[0], rc):
        wide = jnp.dot(src_ref[r0:r0 + rc, :], spread, preferred_element_type=F32)
        keep = ((r0 + rr) // row_group_div) % groups == cg
        dst_ref[r0:r0 + rc, :] = jnp.where(keep, wide, 0.0).astype(dst_ref.dtype)


def _s5_core_kernel(ut_ref, wzc_ref, wyc_ref, alr_ref, ali_ref, d_ref, o_ref,
                    wz_ref, wy_ref, zr_ref, zi_ref, hr_ref, hi_ref, *, l, nk, gc, p):
    @pl.when(pl.program_id(1) == 0)
    def _():
        n_in = l * LANES
        _spread_groups(wzc_ref, wz_ref, gc, p)
        _spread_groups(wyc_ref.at[:n_in], wy_ref.at[:n_in], gc, gc)
        _spread_groups(wyc_ref.at[n_in:], wy_ref.at[n_in:], p, gc)

    us = [ut_ref[pl.ds(j, nk, stride=l), :] for j in range(l)]
    ub = jnp.concatenate([u.astype(BF16) for u in us], axis=1)
    z = jnp.dot(ub, wz_ref[...], preferred_element_type=F32)
    half = z.shape[1] // 2
    zr_ref[...] = z[:, :half]
    zi_ref[...] = z[:, half:]
    alr = alr_ref[...]
    ali = ali_ref[...]
    rows = lax.broadcasted_iota(jnp.int32, (SUBLANES, half), 0)

    def body(kb, c):
        hr, hi = c
        base = pl.multiple_of(kb * SUBLANES, SUBLANES)
        zr = zr_ref[pl.ds(base, SUBLANES), :]
        zi = zi_ref[pl.ds(base, SUBLANES), :]
        out_r = jnp.zeros((SUBLANES, half), F32)
        out_i = jnp.zeros((SUBLANES, half), F32)
        for r in range(SUBLANES):
            out_r = jnp.where(rows == r, hr, out_r)
            out_i = jnp.where(rows == r, hi, out_i)
            hr, hi = (alr * hr - ali * hi + zr[r:r + 1, :], alr * hi + ali * hr + zi[r:r + 1, :])
        hr_ref[pl.ds(base, SUBLANES), :] = out_r
        hi_ref[pl.ds(base, SUBLANES), :] = out_i
        return hr, hi

    zero = jnp.zeros((1, half), F32)
    lax.fori_loop(0, nk // SUBLANES, body, (zero, zero))

    lhs = jnp.concatenate([ub, hr_ref[...].astype(BF16), hi_ref[...].astype(BF16)], axis=1)
    y = jnp.dot(lhs, wy_ref[...], preferred_element_type=F32)
    d = d_ref[...]
    for i in range(l):
        yi = y[:, i * LANES:(i + 1) * LANES] + d * us[i]
        o_ref[pl.ds(i, nk, stride=l), :] = _gelu_tanh(yi).astype(o_ref.dtype)


def _s5_core(u, tables, d_skip):
    b, s, d = u.shape
    l = S5_CHUNK
    nk = s // l
    nt = d // LANES
    wz, wy, alr, ali = tables
    half = alr.shape[2]
    gpt = LANES // S5_GROUP
    return pl.pallas_call(
        functools.partial(_s5_core_kernel, l=l, nk=nk, gc=S5_GROUP, p=half // gpt),
        grid=(nt, b),
        in_specs=[
            pl.BlockSpec((None, s, LANES), lambda t, b: (b, 0, t)),
            pl.BlockSpec((None,) + wz.shape[1:], lambda t, b: (t, 0, 0)),
            pl.BlockSpec((None,) + wy.shape[1:], lambda t, b: (t, 0, 0)),
            pl.BlockSpec((None, 1, half), lambda t, b: (t, 0, 0)),
            pl.BlockSpec((None, 1, half), lambda t, b: (t, 0, 0)),
            pl.BlockSpec((None, 1, LANES), lambda t, b: (t, 0, 0)),
        ],
        out_specs=pl.BlockSpec((None, s, LANES), lambda t, b: (b, 0, t)),
        out_shape=jax.ShapeDtypeStruct((b, s, d), F32),
        scratch_shapes=[pltpu.VMEM((l * LANES, 2 * half), BF16),
                        pltpu.VMEM((l * LANES + 2 * half, l * LANES), BF16)] + [pltpu.VMEM((nk, half), F32)] * 4,
        compiler_params=_cparams(("arbitrary", "arbitrary")),
        name="s5_core",
    )(u, wz, wy, alr, ali, d_skip.astype(F32).reshape(nt, 1, LANES))


def _rope_tab_kernel(pos_ref, f_head_ref, f_idx_ref, ch_ref, sh_ref, ci_ref, si_ref):
    p = pos_ref[...].astype(F32)
    lane = lax.broadcasted_iota(jnp.int32, ch_ref.shape, 1)
    a = p * f_head_ref[...]
    ch_ref[...] = jnp.cos(a)
    sh_ref[...] = jnp.where(lane < HEAD_DIM // 2, -jnp.sin(a), jnp.sin(a))
    a = p * f_idx_ref[...]
    ci_ref[...] = jnp.cos(a)
    si_ref[...] = jnp.where(lane % IDX_DIM < IDX_DIM // 2, -jnp.sin(a), jnp.sin(a))


def _rope_tables(positions, tm=1024):
    b, s = positions.shape

    def inv_freq(dim):
        return ROPE_THETA ** (-jnp.arange(0, dim, 2, dtype=F32) / dim)

    f_head = jnp.tile(inv_freq(HEAD_DIM), 2).reshape(1, LANES)
    f_idx = jnp.tile(inv_freq(IDX_DIM), 2 * LANES // IDX_DIM).reshape(1, LANES)
    row_spec = pl.BlockSpec((None, tm, LANES), lambda b, i: (b, i, 0))
    return pl.pallas_call(
        _rope_tab_kernel,
        grid=(b, s // tm),
        in_specs=[
            pl.BlockSpec((None, tm, 1), lambda b, i: (b, i, 0)),
            pl.BlockSpec((1, LANES), lambda b, i: (0, 0)),
            pl.BlockSpec((1, LANES), lambda b, i: (0, 0)),
        ],
        out_specs=[row_spec] * 4,
        out_shape=[jax.ShapeDtypeStruct((b, s, LANES), F32)] * 4,
        compiler_params=_cparams(("parallel", "parallel")),
        name="rope_tables",
    )(positions.reshape(b, s, 1), f_head, f_idx)


def _rope_head(y, cos, sin_signed):
    return y * cos + pltpu.roll(y, HEAD_DIM // 2, axis=1) * sin_signed


def _rope_idx(y, cos, sin_signed):
    lane = lax.broadcasted_iota(jnp.int32, y.shape, 1)
    half = IDX_DIM // 2
    partner = jnp.where(lane % IDX_DIM < half, pltpu.roll(y, LANES - half, axis=1), pltpu.roll(y, half, axis=1))
    return y * cos + partner * sin_signed


def _dsa_inproj_kernel(x_ref, g_ref, sh_ref, sc_ref, w_ref, wt_ref, gain_ref, ch_ref, shd_ref, ci_ref, si_ref,
                       o_ref, tail_ref, h_ref, *, n_qk, n_v, n_qi):
    j = pl.program_id(2)

    @pl.when(j == 0)
    def _():
        _rms_mod_store(h_ref, x_ref, g_ref, sh_ref, sc_ref)

    y = jnp.dot(h_ref[...], w_ref[...], preferred_element_type=F32)
    slices = [slice(hh * LANES, (hh + 1) * LANES) for hh in range(y.shape[1] // LANES)]

    @pl.when(j < n_qk)
    def _():
        for sl in slices:
            yn = _head_rmsnorm(y[:, sl], gain_ref[:, sl])
            o_ref[:, sl] = _rope_head(yn, ch_ref[...], shd_ref[...]).astype(o_ref.dtype)

    @pl.when((j >= n_qk) & (j < n_qk + n_v))
    def _():
        o_ref[...] = y.astype(o_ref.dtype)

    @pl.when((j >= n_qk + n_v) & (j < n_qk + n_v + n_qi))
    def _():
        for sl in slices:
            o_ref[:, sl] = (_rope_idx(y[:, sl], ci_ref[...], si_ref[...]) * IDX_DIM ** -0.5).astype(o_ref.dtype)

    @pl.when(j == n_qk + n_v + n_qi - 1)
    def _():
        t = jnp.dot(h_ref[...], wt_ref[...], preferred_element_type=F32)
        lane = lax.broadcasted_iota(jnp.int32, t.shape, 1)
        tail_ref[...] = jnp.where(lane < IDX_DIM, _rope_idx(t, ci_ref[...], si_ref[...]), t * IDX_HEADS ** -0.5)


def _dsa_inproj(x, ln_g, mod3, w_main, w_tail, gain_row, tabs, n_heads, tm=1024, tn=512):
    b, s, d = x.shape
    dm = n_heads * HEAD_DIM
    n_qk, n_v, n_qi = 2 * dm // tn, dm // tn, IDX_HEADS * IDX_DIM // tn
    n_main = n_qk + n_v + n_qi
    tab_spec = pl.BlockSpec((None, tm, LANES), lambda b, i, j: (b, i, 0))
    return pl.pallas_call(
        functools.partial(_dsa_inproj_kernel, n_qk=n_qk, n_v=n_v, n_qi=n_qi),
        grid=(b, s // tm, n_main),
        in_specs=[
            pl.BlockSpec((None, tm, d), lambda b, i, j: (b, i, 0)),
            pl.BlockSpec((1, d), lambda b, i, j: (0, 0)),
            *_mod_specs(d, 0, 1),
            pl.BlockSpec((d, tn), lambda b, i, j: (0, j)),
            pl.BlockSpec((d, LANES), lambda b, i, j: (0, 0)),
            pl.BlockSpec((1, tn), lambda b, i, j: (0, j)),
            tab_spec, tab_spec, tab_spec, tab_spec,
        ],
        out_specs=[
            pl.BlockSpec((None, tm, tn), lambda b, i, j: (b, i, j)),
            pl.BlockSpec((None, tm, LANES), lambda b, i, j: (b, i, 0)),
        ],
        out_shape=[jax.ShapeDtypeStruct((b, s, n_main * tn), BF16),
                   jax.ShapeDtypeStruct((b, s, LANES), F32)],
        scratch_shapes=[pltpu.VMEM((tm, d), BF16)],
        compiler_params=_cparams(("parallel", "parallel", "arbitrary")),
        name="dsa_inproj",
    )(x, ln_g.reshape(1, d), mod3, mod3, w_main, w_tail, gain_row, *tabs)


def _dsa_core_kernel(q_ref, k_ref, v_ref, qi_ref, ki_ref, wi_ref, o_ref, key_ref, bias_ref, s_scr, p_scr,
                     *, t, topk, idx_bits, sub):
    i = pl.program_id(1)
    h = pl.program_id(2)
    nk = i + 1
    kf = float(topk)

    def chunk(c):
        return pl.ds(pl.multiple_of(c * t, t), t)

    @pl.when(h == 0)
    def _():
        qi = qi_ref[...]
        wi_t = wi_ref[...].T
        row = lax.broadcasted_iota(jnp.int32, (t, t), 0)
        col = lax.broadcasted_iota(jnp.int32, (t, t), 1)

        def score_body(c, _):
            ki = ki_ref[chunk(c), :][:, :IDX_DIM].astype(BF16)
            score = jnp.zeros((t, t), F32)
            for hh in range(IDX_HEADS):
                rel = lax.dot_general(ki, qi[:, hh * IDX_DIM:(hh + 1) * IDX_DIM], (((1,), (1,)), ((), ())),
                                      preferred_element_type=F32)
                score = score + wi_t[IDX_DIM + hh:IDX_DIM + hh + 1, :] * jnp.maximum(rel, 0.0)
            bits = pltpu.bitcast(score + 0.0, jnp.int32)
            key = jnp.where(bits < 0, bits ^ jnp.int32(0x7FFFFFFF), bits)
            key = jnp.where((c - i) * t + row <= col, key, jnp.int32(INT_MIN))
            key_ref[chunk(c), :] = key
            return 0

        lax.fori_loop(0, nk, score_body, 0)

        sub_row = lax.broadcasted_iota(jnp.int32, (sub, t), 0)

        def count(pred, *query_args):
            def body(c, acc):
                for r in range(t // sub):
                    start = pl.multiple_of(c * t + r * sub, sub)
                    hit = pred(key_ref[pl.ds(start, sub), :], start + sub_row, *query_args)
                    ones = jnp.where(hit, 1.0, 0.0)
                    acc = acc + jnp.sum(ones.reshape(sub // SUBLANES, SUBLANES, t), axis=0)
                return acc
            acc = lax.fori_loop(0, nk, body, jnp.zeros((SUBLANES, t), F32))
            return jnp.sum(acc, axis=0, keepdims=True)

        def bisect(n, thr):
            cand = thr + lax.shift_left(jnp.int32(1), 31 - n)
            return jnp.where(count(lambda key, pos, cd: key >= cd, cand) >= kf, cand, thr)

        thr = lax.fori_loop(0, 32, bisect, jnp.full((1, t), INT_MIN, jnp.int32))
        thr = jnp.maximum(thr, jnp.int32(INT_MIN + 1))
        n_ge = count(lambda key, pos, th: key >= th, thr)

        def tie_cut(_):
            need = kf - count(lambda key, pos, th: key > th, thr)
            def step(n, cut):
                cand = cut + lax.shift_left(jnp.int32(1), idx_bits - 1 - n)
                below = count(lambda key, pos, th, cd: (key == th) & (pos < cd), thr, cand)
                return jnp.where(below < need, cand, cut)
            return lax.fori_loop(0, idx_bits, step, jnp.zeros((1, t), jnp.int32))

        cut = lax.cond(jnp.max(n_ge) > kf, tie_cut, lambda _: jnp.full((1, t), 2 ** idx_bits - 1, jnp.int32), 0)

        def bias_body(c, _):
            key = key_ref[chunk(c), :]
            sel = (key > thr) | ((key == thr) & (c * t + row <= cut))
            bias_ref[chunk(c), :] = jnp.where(sel, 0.0, NEG_BIG)
            return 0

        lax.fori_loop(0, nk, bias_body, 0)

    q = q_ref[...]

    def logits(c):
        kc = k_ref[chunk(jnp.minimum(c, nk - 1)), :]
        return lax.dot_general(kc, q, (((1,), (1,)), ((), ())), preferred_element_type=F32)

    def weights(c, s, m, l):
        s = s + bias_ref[chunk(c), :]
        m_new = jnp.maximum(m, jnp.max(s, axis=0, keepdims=True))
        alpha = jnp.exp2(m - m_new)
        p = jnp.exp2(s - m_new)
        return p.astype(BF16), alpha, m_new, alpha * l + jnp.sum(p, axis=0, keepdims=True)

    def values(c, p, alpha, acc):
        pv = lax.dot_general(v_ref[chunk(c), :], p, (((0,), (0,)), ((), ())), preferred_element_type=F32)
        return alpha * acc + pv

    def step(n, st, slot):
        alpha, m, l, acc = st
        s_scr[1 - slot] = logits(n + 2)
        acc = values(n, p_scr[slot], alpha, acc)
        p, alpha, m, l = weights(n + 1, s_scr[slot], m, l)
        p_scr[1 - slot] = p
        return alpha, m, l, acc

    p0, alpha0, m0, l0 = weights(0, logits(0), jnp.full((1, t), NEG_BIG, F32), jnp.zeros((1, t), F32))
    p_scr[0] = p0
    s_scr[0] = logits(1)
    n_steps = nk - 1
    st = (alpha0, m0, l0, jnp.zeros((HEAD_DIM, t), F32))
    st = lax.fori_loop(0, lax.shift_right_logical(n_steps, 1),
                       lambda pp, c: step(2 * pp + 1, step(2 * pp, c, 0), 1), st)
    alpha, _, l, acc = lax.cond((n_steps & 1) == 1, lambda c: step(n_steps - 1, c, 0), lambda c: c, st)
    acc = values(nk - 1, p_scr[n_steps & 1], alpha, acc)
    o_ref[...] = (acc / l).T.astype(o_ref.dtype)


def _dsa_core(main, tail, n_heads, t=512):
    b, s, _ = main.shape
    dm = n_heads * HEAD_DIM
    topk = min(DSA_TOPK_MAX, s // 4)
    idx_bits = max(1, (s - 1).bit_length())
    qi_block = (3 * dm) // (IDX_HEADS * IDX_DIM)
    return pl.pallas_call(
        functools.partial(_dsa_core_kernel, t=t, topk=topk, idx_bits=idx_bits, sub=min(t, 64)),
        grid=(b, s // t, n_heads),
        in_specs=[
            pl.BlockSpec((None, t, HEAD_DIM), lambda b, i, h: (b, i, h)),
            pl.BlockSpec((None, s, HEAD_DIM), lambda b, i, h: (b, 0, n_heads + h)),
            pl.BlockSpec((None, s, HEAD_DIM), lambda b, i, h: (b, 0, 2 * n_heads + h)),
            pl.BlockSpec((None, t, IDX_HEADS * IDX_DIM), lambda b, i, h: (b, i, qi_block)),
            pl.BlockSpec((None, s, LANES), lambda b, i, h: (b, 0, 0)),
            pl.BlockSpec((None, t, LANES), lambda b, i, h: (b, i, 0)),
        ],
        out_specs=pl.BlockSpec((None, t, HEAD_DIM), lambda b, i, h: (b, i, h)),
        out_shape=jax.ShapeDtypeStruct((b, s, dm), BF16),
        scratch_shapes=[pltpu.VMEM((s, t), jnp.int32), pltpu.VMEM((s, t), F32),
                        pltpu.VMEM((2, t, t), F32), pltpu.VMEM((2, t, t), BF16)],
        compiler_params=_cparams(("parallel", "parallel", "arbitrary")),
        name="dsa_core",
    )(main, main, main, main, tail, tail)


def kernel(x, c, positions, ln1_g, ln2_g, ada_w, ada_b, mlp_w1, mlp_w2, sb_w_in, sb_q_gain, sb_k_gain, sb_w_out, s5_w_in, s5_lambda_re, s5_lambda_im, s5_log_dt, s5_b_re, s5_b_im, s5_c_re, s5_c_im, s5_d, s5_w_glu, dsa_w_in, dsa_q_gain, dsa_k_gain, dsa_w_out):
    depth = ada_w.shape[0]
    b, s, d = x.shape
    n_heads = d // HEAD_DIM
    scale = HEAD_DIM ** -0.5

    mod = _ada_mod(c, ada_w, ada_b)
    rope_tabs = _rope_tables(positions) if depth > 2 else None
    mlp_w1, mlp_w2, sb_w_in, sb_w_out, s5_w_in, s5_w_glu, dsa_w_out = (
        w.astype(BF16) for w in (mlp_w1, mlp_w2, sb_w_in, sb_w_out, s5_w_in, s5_w_glu, dsa_w_out))

    counts = [0, 0, 0]
    for i in range(depth):
        mod3 = mod[i].reshape(SUBLANES, 1, 6 * d)
        kind = i % N_MIXERS
        j = counts[kind]
        counts[kind] += 1
        if kind == 0:
            gain = jnp.concatenate([jnp.tile(sb_q_gain[j].astype(F32) * (scale * LOG2_E), n_heads),
                                    jnp.tile(sb_k_gain[j].astype(F32), n_heads),
                                    jnp.ones((d,), F32)]).reshape(1, 3 * d)
            qkv = _sb_inproj(x, ln1_g[i], mod3, sb_w_in, j, gain)
            o = _sb_attention(qkv, n_heads)
            x = _out_gate(o, sb_w_out, j, x, mod3, 2)
        elif kind == 1:
            u = _nm_matmul(x, ln1_g[i], mod3, s5_w_in, j, F32)
            tables = _s5_tables(s5_lambda_re[j], s5_lambda_im[j], s5_log_dt[j], s5_b_re[j], s5_b_im[j],
                                s5_c_re[j], s5_c_im[j])
            z = _s5_core(u, tables, s5_d[j])
            x = _glu_gate(z, s5_w_glu, j, x, mod3, 2)
        else:
            tn = 512
            n_in = dsa_w_in.shape[2]
            n_main = 3 * d + IDX_HEADS * IDX_DIM
            w_in = dsa_w_in[j].astype(BF16)
            w_tail = jnp.zeros((d, LANES), BF16).at[:, :n_in - n_main].set(w_in[:, n_main:])
            gain = jnp.zeros((1, n_main), F32)
            gain = gain.at[0, :d].set(jnp.tile(dsa_q_gain[j].astype(F32) * (scale * LOG2_E), n_heads))
            gain = gain.at[0, d:2 * d].set(jnp.tile(dsa_k_gain[j].astype(F32), n_heads))
            main, tail = _dsa_inproj(x, ln1_g[i], mod3, w_in[:, :n_main], w_tail, gain, rope_tabs, n_heads, tn=tn)
            o = _dsa_core(main, tail, n_heads)
            x = _out_gate(o, dsa_w_out, j, x, mod3, 2)
        x = _mlp(x, ln2_g[i], mod3, mlp_w1, mlp_w2, i)
    return x
```

```python
import functools
import math

import jax
import jax.numpy as jnp
from jax import lax
from jax.experimental import pallas as pl
from jax.experimental.pallas import tpu as pltpu

F32 = jnp.float32
BF16 = jnp.bfloat16

N_MIXERS = 3
HEAD_DIM = 128
ROPE_THETA = 10000.0
EPS = 1e-6
S5_GROUP = 16
S5_STATE = 64
IDX_HEADS = 16
IDX_DIM = 64
DSA_TOPK_MAX = 256

LANES = 128
SUBLANES = 8
ROW_STRIP = 16
VMEM_LIMIT_BYTES = 56 * 1024 * 1024

S5_CHUNK = 16
NEG_BIG = -1e30
INT_MIN = -(2 ** 31)
LOG2_E = 1.4426950408889634


def _cparams(sem):
    return pltpu.CompilerParams(dimension_semantics=sem, vmem_limit_bytes=VMEM_LIMIT_BYTES)


def _sigmoid(x):
    return 1.0 / (1.0 + jnp.exp(-x))


def _gelu_tanh(x):
    c = math.sqrt(2.0 / math.pi)
    return 0.5 * x * (1.0 + jnp.tanh(c * (x + 0.044715 * (x * x * x))))


def _rms_mod_store(h_ref, x_ref, g_ref, sh_ref, sc_ref):
    d = x_ref.shape[1]
    gain = jnp.broadcast_to(g_ref[...] * (1.0 + sc_ref[...]), (ROW_STRIP, d))
    shift = jnp.broadcast_to(sh_ref[...], (ROW_STRIP, d))

    def body(r, _):
        rows = pl.ds(pl.multiple_of(r * ROW_STRIP, ROW_STRIP), ROW_STRIP)
        x = x_ref[rows, :]
        ms = jnp.mean(x * x, axis=-1, keepdims=True)
        h_ref[rows, :] = (x * lax.rsqrt(ms + EPS) * gain + shift).astype(h_ref.dtype)
        return 0

    lax.fori_loop(0, x_ref.shape[0] // ROW_STRIP, body, 0, unroll=8)


def _head_rmsnorm(y, gain):
    ms = jnp.mean(y * y, axis=-1, keepdims=True)
    return y * lax.rsqrt(ms + EPS) * gain


def _ada_kernel(c_ref, w_ref, b_ref, o_ref):
    c = c_ref[...]
    cond = (c * _sigmoid(c)).astype(BF16)
    o_ref[...] = jnp.dot(cond, w_ref[...].astype(BF16), preferred_element_type=F32) + b_ref[...]


def _ada_mod(c, ada_w, ada_b):
    depth, d, n = ada_w.shape
    b = c.shape[0]
    c8 = jnp.zeros((SUBLANES, d), F32).at[:b].set(c)
    tn = 1024
    return pl.pallas_call(
        _ada_kernel,
        grid=(depth, n // tn),
        in_specs=[
            pl.BlockSpec((SUBLANES, d), lambda l, j: (0, 0)),
            pl.BlockSpec((None, d, tn), lambda l, j: (l, 0, j)),
            pl.BlockSpec((None, 1, tn), lambda l, j: (l, 0, j)),
        ],
        out_specs=pl.BlockSpec((None, SUBLANES, tn), lambda l, j: (l, 0, j)),
        out_shape=jax.ShapeDtypeStruct((depth, SUBLANES, n), F32),
        compiler_params=_cparams(("parallel", "parallel")),
        name="ada_mod",
    )(c8, ada_w, ada_b.reshape(depth, 1, n))


def _mod_specs(d, shift_chunk, scale_chunk):
    return [
        pl.BlockSpec((None, 1, d), lambda b, i, j: (b, 0, shift_chunk)),
        pl.BlockSpec((None, 1, d), lambda b, i, j: (b, 0, scale_chunk)),
    ]


def _nm_matmul_kernel(x_ref, g_ref, sh_ref, sc_ref, w_ref, o_ref, h_ref):
    @pl.when(pl.program_id(2) == 0)
    def _():
        _rms_mod_store(h_ref, x_ref, g_ref, sh_ref, sc_ref)

    o_ref[...] = jnp.dot(h_ref[...], w_ref[...], preferred_element_type=F32).astype(o_ref.dtype)


def _nm_matmul(x, ln_g, mod3, w, layer, out_dtype, tm=1024, tn=1024):
    b, s, d = x.shape
    n = w.shape[2]
    return pl.pallas_call(
        _nm_matmul_kernel,
        grid=(b, s // tm, n // tn),
        in_specs=[
            pl.BlockSpec((None, tm, d), lambda b, i, j: (b, i, 0)),
            pl.BlockSpec((1, d), lambda b, i, j: (0, 0)),
            *_mod_specs(d, 0, 1),
            pl.BlockSpec((None, d, tn), lambda b, i, j: (layer, 0, j)),
        ],
        out_specs=pl.BlockSpec((None, tm, tn), lambda b, i, j: (b, i, j)),
        out_shape=jax.ShapeDtypeStruct((b, s, n), out_dtype),
        scratch_shapes=[pltpu.VMEM((tm, d), BF16)],
        compiler_params=_cparams(("parallel", "parallel", "arbitrary")),
        name="nm_matmul",
    )(x, ln_g.reshape(1, d), mod3, mod3, w)


def _sb_inproj_kernel(x_ref, g_ref, sh_ref, sc_ref, w_ref, gain_ref, o_ref, h_ref, *, n_norm_tiles):
    j = pl.program_id(2)

    @pl.when(j == 0)
    def _():
        _rms_mod_store(h_ref, x_ref, g_ref, sh_ref, sc_ref)

    y = jnp.dot(h_ref[...], w_ref[...], preferred_element_type=F32)

    @pl.when(j < n_norm_tiles)
    def _():
        for hh in range(y.shape[1] // HEAD_DIM):
            sl = slice(hh * HEAD_DIM, (hh + 1) * HEAD_DIM)
            o_ref[:, sl] = _head_rmsnorm(y[:, sl], gain_ref[:, sl]).astype(o_ref.dtype)

    @pl.when(j >= n_norm_tiles)
    def _():
        o_ref[...] = y.astype(o_ref.dtype)


def _sb_inproj(x, ln_g, mod3, w, layer, gain_row, tm=1024, tn=1024):
    b, s, d = x.shape
    n = w.shape[2]
    return pl.pallas_call(
        functools.partial(_sb_inproj_kernel, n_norm_tiles=2 * d // tn),
        grid=(b, s // tm, n // tn),
        in_specs=[
            pl.BlockSpec((None, tm, d), lambda b, i, j: (b, i, 0)),
            pl.BlockSpec((1, d), lambda b, i, j: (0, 0)),
            *_mod_specs(d, 0, 1),
            pl.BlockSpec((None, d, tn), lambda b, i, j: (layer, 0, j)),
            pl.BlockSpec((1, tn), lambda b, i, j: (0, j)),
        ],
        out_specs=pl.BlockSpec((None, tm, tn), lambda b, i, j: (b, i, j)),
        out_shape=jax.ShapeDtypeStruct((b, s, n), BF16),
        scratch_shapes=[pltpu.VMEM((tm, d), BF16)],
        compiler_params=_cparams(("parallel", "parallel", "arbitrary")),
        name="sb_inproj",
    )(x, ln_g.reshape(1, d), mod3, mod3, w, gain_row)


def _out_gate_kernel(a_ref, w_ref, x_ref, gate_ref, o_ref):
    y = jnp.dot(a_ref[...], w_ref[...], preferred_element_type=F32)
    o_ref[...] = x_ref[...] + gate_ref[...] * y


def _out_gate(a, w, layer, x, mod3, gate_chunk, tm=1024, tn=1024):
    b, s, d = x.shape
    k = a.shape[2]
    nt = d // tn
    return pl.pallas_call(
        _out_gate_kernel,
        grid=(b, s // tm, nt),
        in_specs=[
            pl.BlockSpec((None, tm, k), lambda b, i, j: (b, i, 0)),
            pl.BlockSpec((None, k, tn), lambda b, i, j: (layer, 0, j)),
            pl.BlockSpec((None, tm, tn), lambda b, i, j: (b, i, j)),
            pl.BlockSpec((None, 1, tn), lambda b, i, j: (b, 0, gate_chunk * nt + j)),
        ],
        out_specs=pl.BlockSpec((None, tm, tn), lambda b, i, j: (b, i, j)),
        out_shape=jax.ShapeDtypeStruct((b, s, d), F32),
        compiler_params=_cparams(("parallel", "parallel", "parallel")),
        name="out_gate",
    )(a, w, x, mod3)


def _glu_gate_kernel(z_ref, wa_ref, wg_ref, x_ref, gate_ref, o_ref, zb_ref):
    @pl.when(pl.program_id(2) == 0)
    def _():
        zb_ref[...] = z_ref[...].astype(BF16)

    z = zb_ref[...]
    a = jnp.dot(z, wa_ref[...], preferred_element_type=F32)
    g = jnp.dot(z, wg_ref[...], preferred_element_type=F32)
    o_ref[...] = x_ref[...] + gate_ref[...] * (a * _sigmoid(g))


def _glu_gate(z, w_glu, layer, x, mod3, gate_chunk, tm=1024, tn=512):
    b, s, d = x.shape
    nt = d // tn
    return pl.pallas_call(
        _glu_gate_kernel,
        grid=(b, s // tm, nt),
        in_specs=[
            pl.BlockSpec((None, tm, d), lambda b, i, j: (b, i, 0)),
            pl.BlockSpec((None, d, tn), lambda b, i, j: (layer, 0, j)),
            pl.BlockSpec((None, d, tn), lambda b, i, j: (layer, 0, nt + j)),
            pl.BlockSpec((None, tm, tn), lambda b, i, j: (b, i, j)),
            pl.BlockSpec((None, 1, tn), lambda b, i, j: (b, 0, gate_chunk * nt + j)),
        ],
        out_specs=pl.BlockSpec((None, tm, tn), lambda b, i, j: (b, i, j)),
        out_shape=jax.ShapeDtypeStruct((b, s, d), F32),
        scratch_shapes=[pltpu.VMEM((tm, d), BF16)],
        compiler_params=_cparams(("parallel", "parallel", "arbitrary")),
        name="glu_gate",
    )(z, w_glu, w_glu, x, mod3)


def _mlp_kernel(x_ref, g_ref, sh_ref, sc_ref, w1_ref, w2_ref, gate_ref, o_ref, h_ref, acc_ref):
    j = pl.program_id(2)

    @pl.when(j == 0)
    def _():
        _rms_mod_store(h_ref, x_ref, g_ref, sh_ref, sc_ref)
        acc_ref[...] = jnp.zeros_like(acc_ref)

    a = jnp.maximum(jnp.dot(h_ref[...], w1_ref[...], preferred_element_type=F32), 0.0)
    acc_ref[...] += jnp.dot((a * a).astype(BF16), w2_ref[...], preferred_element_type=F32)

    @pl.when(j == pl.num_programs(2) - 1)
    def _():
        o_ref[...] = x_ref[...] + gate_ref[...] * acc_ref[...]


def _mlp(x, ln_g, mod3, w1, w2, layer, tm=512, tf=1024):
    b, s, d = x.shape
    f = w1.shape[2]
    return pl.pallas_call(
        _mlp_kernel,
        grid=(b, s // tm, f // tf),
        in_specs=[
            pl.BlockSpec((None, tm, d), lambda b, i, j: (b, i, 0)),
            pl.BlockSpec((1, d), lambda b, i, j: (0, 0)),
            *_mod_specs(d, 3, 4),
            pl.BlockSpec((None, d, tf), lambda b, i, j: (layer, 0, j)),
            pl.BlockSpec((None, tf, d), lambda b, i, j: (layer, j, 0)),
            pl.BlockSpec((None, 1, d), lambda b, i, j: (b, 0, 5)),
        ],
        out_specs=pl.BlockSpec((None, tm, d), lambda b, i, j: (b, i, 0)),
        out_shape=jax.ShapeDtypeStruct((b, s, d), F32),
        scratch_shapes=[pltpu.VMEM((tm, d), BF16), pltpu.VMEM((tm, d), F32)],
        compiler_params=_cparams(("parallel", "parallel", "arbitrary")),
        name="mlp",
    )(x, ln_g.reshape(1, d), mod3, mod3, w1, w2, mod3)


def _sb_attn_kernel(q_ref, k_ref, v_ref, later_ref, o_ref, z_scr, e_scr, *, tq, tk, hps):
    i = pl.program_id(2)
    m = (i + 1) * (tq // tk)
    n_diag = tq // tk
    heads = range(hps)
    cols = [slice(hh * HEAD_DIM, (hh + 1) * HEAD_DIM) for hh in heads]
    qs = [q_ref[:, cols[hh]] for hh in heads]
    lead = lax.broadcasted_iota(jnp.int32, (tq, tk), 0) - lax.broadcasted_iota(jnp.int32, (tq, tk), 1)
    sign = jnp.uint32(0x80000000)

    def key_start(n):
        return pl.multiple_of(jnp.maximum(m - 1 - n, 0) * tk, tk)

    def score(hh, n):
        kb = k_ref[pl.ds(key_start(n), tk), cols[hh]]
        return lax.dot_general(qs[hh], kb, (((1,), (1,)), ((), ())), preferred_element_type=F32)

    def log_weights(n, z, masked):
        neg_abs = pltpu.bitcast(pltpu.bitcast(z, jnp.uint32) | sign, F32)
        ls = jnp.minimum(z, 0.0) - jnp.log(1.0 + jnp.exp2(neg_abs)) * LOG2_E
        ln = ls - z
        if masked:
            causal = key_start(n) - i * tq < lead
            ln = jnp.where(causal, ln, 0.0)
            ls = jnp.where(causal, ls, NEG_BIG)
        ln = ln.astype(BF16)
        later = jnp.dot(ln, later_ref[...], preferred_element_type=F32)
        return ls + later, later[:, 0:1] + ln[:, 0:1].astype(F32)

    def accumulate(hh, n, e, carry, acc):
        vb = v_ref[pl.ds(key_start(n), tk), cols[hh]]
        w = jnp.exp2(e + carry).astype(BF16)
        return acc + jnp.dot(w, vb, preferred_element_type=F32)

    states = []
    for hh in heads:
        e0, tot0 = log_weights(0, score(hh, 0), True)
        e_scr[hh, 0] = e0
        z_scr[hh, 0] = score(hh, 1)
        states.append((jnp.zeros((tq, 1), F32), tot0, jnp.zeros((tq, HEAD_DIM), F32)))

    def step(n, sts):
        slot = n % 2
        out = []
        for hh in heads:
            carry, tot, acc = sts[hh]
            z_scr[hh, 1 - slot] = score(hh, n + 2)
            acc = accumulate(hh, n, e_scr[hh, slot], carry, acc)
            e_next, tot_next = log_weights(n + 1, z_scr[hh, slot], True)
            e_scr[hh, 1 - slot] = e_next
            out.append((carry + tot, tot_next, acc))
        return out

    def pair(p, sts):
        n = n_diag - 1 + 2 * p
        z_mid = [score(hh, n + 2) for hh in heads]
        accs = [accumulate(hh, n, e_scr[hh, 1], sts[hh][0], sts[hh][2]) for hh in heads]
        mids = [log_weights(n + 1, z_scr[hh, 1], False) for hh in heads]
        carries = [sts[hh][0] + sts[hh][1] for hh in heads]
        for hh in heads:
            z_scr[hh, 1] = score(hh, n + 3)
        accs = [accumulate(hh, n + 1, mids[hh][0], carries[hh], accs[hh]) for hh in heads]
        lasts = [log_weights(n + 2, z_mid[hh], False) for hh in heads]
        for hh in heads:
            e_scr[hh, 1] = lasts[hh][0]
        return tuple((carries[hh] + mids[hh][1], lasts[hh][1], accs[hh]) for hh in heads)

    assert n_diag % 2 == 0
    for n in range(n_diag - 1):
        states = step(n, states)
    n_pairs = (i * n_diag) // 2
    n_quads = lax.shift_right_logical(n_pairs, 1)
    states = lax.fori_loop(0, n_quads, lambda qd, c: pair(2 * qd + 1, pair(2 * qd, c)), tuple(states))
    states = lax.fori_loop(2 * n_quads, n_pairs, pair, states)
    for hh in heads:
        carry, _, acc = states[hh]
        o_ref[:, cols[hh]] = accumulate(hh, m - 1, e_scr[hh, 1], carry, acc).astype(o_ref.dtype)


def _sb_attention(qkv, n_heads, tq=512, tk=256, hps=1):
    b, s, _ = qkv.shape
    groups = n_heads // hps
    width = hps * HEAD_DIM
    pos = jnp.arange(tk)
    later_mat = (pos[:, None] > pos[None, :]).astype(BF16)
    return pl.pallas_call(
        functools.partial(_sb_attn_kernel, tq=tq, tk=tk, hps=hps),
        grid=(b, groups, s // tq),
        in_specs=[
            pl.BlockSpec((None, tq, width), lambda b, g, i: (b, i, g)),
            pl.BlockSpec((None, s, width), lambda b, g, i: (b, 0, groups + g)),
            pl.BlockSpec((None, s, width), lambda b, g, i: (b, 0, 2 * groups + g)),
            pl.BlockSpec((tk, tk), lambda b, g, i: (0, 0)),
        ],
        out_specs=pl.BlockSpec((None, tq, width), lambda b, g, i: (b, i, g)),
        out_shape=jax.ShapeDtypeStruct((b, s, n_heads * HEAD_DIM), BF16),
        scratch_shapes=[pltpu.VMEM((hps, 2, tq, tk), F32), pltpu.VMEM((hps, 2, tq, tk), F32)],
        compiler_params=_cparams(("parallel", "parallel", "parallel")),
        name="sb_attn",
    )(qkv, qkv, qkv, later_mat)


def _s5_tables(lam_re, lam_im, log_dt, b_re, b_im, c_re, c_im):
    g, p, gc = b_re.shape
    l = S5_CHUNK
    gpt = LANES // gc
    nt = g // gpt
    hi = lax.Precision.HIGHEST
    dt = jnp.exp(log_dt.astype(F32))[:, None]
    lr = lam_re.astype(F32)
    li = lam_im.astype(F32)
    mag = jnp.exp(lr * dt)
    ar = mag * jnp.cos(li * dt)
    ai = mag * jnp.sin(li * dt)
    den = lr * lr + li * li
    fr = ((ar - 1.0) * lr + ai * li) / den
    fi = (ai * lr - (ar - 1.0) * li) / den
    br_ = b_re.astype(F32)
    bi_ = b_im.astype(F32)
    bbr = fr[..., None] * br_ - fi[..., None] * bi_
    bbi = fr[..., None] * bi_ + fi[..., None] * br_
    cr = c_re.astype(F32)
    ci = c_im.astype(F32)
    n = jnp.arange(l + 1, dtype=F32)[:, None, None]
    pw_r = jnp.exp(n * (lr * dt)) * jnp.cos(n * (li * dt))
    pw_i = jnp.exp(n * (lr * dt)) * jnp.sin(n * (li * dt))
    bt_r = bbr.transpose(0, 2, 1)
    bt_i = bbi.transpose(0, 2, 1)
    ab_r = pw_r[:l, :, None, :] * bt_r - pw_i[:l, :, None, :] * bt_i
    ab_i = pw_r[:l, :, None, :] * bt_i + pw_i[:l, :, None, :] * bt_r

    def tile_rows(t):
        cols = t.shape[-1]
        return t.reshape(l, nt, gpt, gc, cols).transpose(1, 0, 2, 3, 4).reshape(nt, l * LANES, cols)

    wz = tile_rows(jnp.concatenate([ab_r[::-1], ab_i[::-1]], axis=-1))

    kern = (jnp.einsum('gcp,ngdp->gdnc', cr, ab_r, precision=hi)
            - jnp.einsum('gcp,ngdp->gdnc', ci, ab_i, precision=hi)).reshape(g, gc, l * gc)
    intra = tile_rows(jnp.stack([jnp.pad(kern[:, :, :(l - j) * gc], ((0, 0), (0, 0), (j * gc, 0)))
                                 for j in range(l)]))

    ct_r = jnp.tile(cr.transpose(0, 2, 1), (1, 1, l))
    ct_i = jnp.tile(ci.transpose(0, 2, 1), (1, 1, l))
    p1_r = jnp.repeat(pw_r[1:].transpose(1, 2, 0), gc, axis=-1)
    p1_i = jnp.repeat(pw_i[1:].transpose(1, 2, 0), gc, axis=-1)
    vr = (ct_r * p1_r - ct_i * p1_i).reshape(nt, gpt * p, l * gc)
    vi = (-(ct_r * p1_i + ct_i * p1_r)).reshape(nt, gpt * p, l * gc)
    wy = jnp.concatenate([intra, vr, vi], axis=1)

    alr = pw_r[l].reshape(nt, 1, gpt * p)
    ali = pw_i[l].reshape(nt, 1, gpt * p)
    return wz.astype(BF16), wy.astype(BF16), alr, ali


def _spread_groups(src_ref, dst_ref, row_group_div, col_unit):
    n_src = src_ref.shape[1]
    n_dst = dst_ref.shape[1]
    groups = n_dst // n_src
    rc = 256
    sr = lax.broadcasted_iota(jnp.int32, (n_src, n_dst), 0)
    dc = lax.broadcasted_iota(jnp.int32, (n_src, n_dst), 1)
    spread = ((sr // col_unit == dc // (col_unit * groups)) & (sr % col_unit == dc % col_unit))
    spread = jnp.where(spread, 1.0, 0.0).astype(BF16)
    rr = lax.broadcasted_iota(jnp.int32, (rc, n_dst), 0)
    cg = (lax.broadcasted_iota(jnp.int32, (rc, n_dst), 1) // col_unit) % groups
    for r0 in range(0, src_ref.shape[0], rc):
        wide = jnp.dot(src_ref[r0:r0 + rc, :], spread, preferred_element_type=F32)
        keep = ((r0 + rr) // row_group_div) % groups == cg
        dst_ref[r0:r0 + rc, :] = jnp.where(keep, wide, 0.0).astype(dst_ref.dtype)


def _s5_core_kernel(ut_ref, wzc_ref, wyc_ref, alr_ref, ali_ref, d_ref, o_ref,
                    wz_ref, wy_ref, zr_ref, zi_ref, hr_ref, hi_ref, *, l, nk, gc, p):
    @pl.when(pl.program_id(1) == 0)
    def _():
        n_in = l * LANES
        _spread_groups(wzc_ref, wz_ref, gc, p)
        _spread_groups(wyc_ref.at[:n_in], wy_ref.at[:n_in], gc, gc)
        _spread_groups(wyc_ref.at[n_in:], wy_ref.at[n_in:], p, gc)

    us = [ut_ref[pl.ds(j, nk, stride=l), :] for j in range(l)]
    ub = jnp.concatenate([u.astype(BF16) for u in us], axis=1)
    z = jnp.dot(ub, wz_ref[...], preferred_element_type=F32)
    half = z.shape[1] // 2
    zr_ref[...] = z[:, :half]
    zi_ref[...] = z[:, half:]
    alr = alr_ref[...]
    ali = ali_ref[...]
    rows = lax.broadcasted_iota(jnp.int32, (SUBLANES, half), 0)

    def body(kb, c):
        hr, hi = c
        base = pl.multiple_of(kb * SUBLANES, SUBLANES)
        zr = zr_ref[pl.ds(base, SUBLANES), :]
        zi = zi_ref[pl.ds(base, SUBLANES), :]
        out_r = jnp.zeros((SUBLANES, half), F32)
        out_i = jnp.zeros((SUBLANES, half), F32)
        for r in range(SUBLANES):
            out_r = jnp.where(rows == r, hr, out_r)
            out_i = jnp.where(rows == r, hi, out_i)
            hr, hi = (alr * hr - ali * hi + zr[r:r + 1, :], alr * hi + ali * hr + zi[r:r + 1, :])
        hr_ref[pl.ds(base, SUBLANES), :] = out_r
        hi_ref[pl.ds(base, SUBLANES), :] = out_i
        return hr, hi

    zero = jnp.zeros((1, half), F32)
    lax.fori_loop(0, nk // SUBLANES, body, (zero, zero))

    lhs = jnp.concatenate([ub, hr_ref[...].astype(BF16), hi_ref[...].astype(BF16)], axis=1)
    y = jnp.dot(lhs, wy_ref[...], preferred_element_type=F32)
    d = d_ref[...]
    for i in range(l):
        yi = y[:, i * LANES:(i + 1) * LANES] + d * us[i]
        o_ref[pl.ds(i, nk, stride=l), :] = _gelu_tanh(yi).astype(o_ref.dtype)


def _s5_core(u, tables, d_skip):
    b, s, d = u.shape
    l = S5_CHUNK
    nk = s // l
    nt = d // LANES
    wz, wy, alr, ali = tables
    half = alr.shape[2]
    gpt = LANES // S5_GROUP
    return pl.pallas_call(
        functools.partial(_s5_core_kernel, l=l, nk=nk, gc=S5_GROUP, p=half // gpt),
        grid=(nt, b),
        in_specs=[
            pl.BlockSpec((None, s, LANES), lambda t, b: (b, 0, t)),
            pl.BlockSpec((None,) + wz.shape[1:], lambda t, b: (t, 0, 0)),
            pl.BlockSpec((None,) + wy.shape[1:], lambda t, b: (t, 0, 0)),
            pl.BlockSpec((None, 1, half), lambda t, b: (t, 0, 0)),
            pl.BlockSpec((None, 1, half), lambda t, b: (t, 0, 0)),
            pl.BlockSpec((None, 1, LANES), lambda t, b: (t, 0, 0)),
        ],
        out_specs=pl.BlockSpec((None, s, LANES), lambda t, b: (b, 0, t)),
        out_shape=jax.ShapeDtypeStruct((b, s, d), F32),
        scratch_shapes=[pltpu.VMEM((l * LANES, 2 * half), BF16),
                        pltpu.VMEM((l * LANES + 2 * half, l * LANES), BF16)] + [pltpu.VMEM((nk, half), F32)] * 4,
        compiler_params=_cparams(("arbitrary", "arbitrary")),
        name="s5_core",
    )(u, wz, wy, alr, ali, d_skip.astype(F32).reshape(nt, 1, LANES))


def _rope_tab_kernel(pos_ref, f_head_ref, f_idx_ref, ch_ref, sh_ref, ci_ref, si_ref):
    p = pos_ref[...].astype(F32)
    lane = lax.broadcasted_iota(jnp.int32, ch_ref.shape, 1)
    a = p * f_head_ref[...]
    ch_ref[...] = jnp.cos(a)
    sh_ref[...] = jnp.where(lane < HEAD_DIM // 2, -jnp.sin(a), jnp.sin(a))
    a = p * f_idx_ref[...]
    ci_ref[...] = jnp.cos(a)
    si_ref[...] = jnp.where(lane % IDX_DIM < IDX_DIM // 2, -jnp.sin(a), jnp.sin(a))


def _rope_tables(positions, tm=1024):
    b, s = positions.shape

    def inv_freq(dim):
        return ROPE_THETA ** (-jnp.arange(0, dim, 2, dtype=F32) / dim)

    f_head = jnp.tile(inv_freq(HEAD_DIM), 2).reshape(1, LANES)
    f_idx = jnp.tile(inv_freq(IDX_DIM), 2 * LANES // IDX_DIM).reshape(1, LANES)
    row_spec = pl.BlockSpec((None, tm, LANES), lambda b, i: (b, i, 0))
    return pl.pallas_call(
        _rope_tab_kernel,
        grid=(b, s // tm),
        in_specs=[
            pl.BlockSpec((None, tm, 1), lambda b, i: (b, i, 0)),
            pl.BlockSpec((1, LANES), lambda b, i: (0, 0)),
            pl.BlockSpec((1, LANES), lambda b, i: (0, 0)),
        ],
        out_specs=[row_spec] * 4,
        out_shape=[jax.ShapeDtypeStruct((b, s, LANES), F32)] * 4,
        compiler_params=_cparams(("parallel", "parallel")),
        name="rope_tables",
    )(positions.reshape(b, s, 1), f_head, f_idx)


def _rope_head(y, cos, sin_signed):
    return y * cos + pltpu.roll(y, HEAD_DIM // 2, axis=1) * sin_signed


def _rope_idx(y, cos, sin_signed):
    lane = lax.broadcasted_iota(jnp.int32, y.shape, 1)
    half = IDX_DIM // 2
    partner = jnp.where(lane % IDX_DIM < half, pltpu.roll(y, LANES - half, axis=1), pltpu.roll(y, half, axis=1))
    return y * cos + partner * sin_signed


def _dsa_inproj_kernel(x_ref, g_ref, sh_ref, sc_ref, w_ref, wt_ref, gain_ref, ch_ref, shd_ref, ci_ref, si_ref,
                       o_ref, tail_ref, h_ref, *, n_qk, n_v, n_qi):
    j = pl.program_id(2)

    @pl.when(j == 0)
    def _():
        _rms_mod_store(h_ref, x_ref, g_ref, sh_ref, sc_ref)

    y = jnp.dot(h_ref[...], w_ref[...], preferred_element_type=F32)
    slices = [slice(hh * LANES, (hh + 1) * LANES) for hh in range(y.shape[1] // LANES)]

    @pl.when(j < n_qk)
    def _():
        for sl in slices:
            yn = _head_rmsnorm(y[:, sl], gain_ref[:, sl])
            o_ref[:, sl] = _rope_head(yn, ch_ref[...], shd_ref[...]).astype(o_ref.dtype)

    @pl.when((j >= n_qk) & (j < n_qk + n_v))
    def _():
        o_ref[...] = y.astype(o_ref.dtype)

    @pl.when((j >= n_qk + n_v) & (j < n_qk + n_v + n_qi))
    def _():
        for sl in slices:
            o_ref[:, sl] = (_rope_idx(y[:, sl], ci_ref[...], si_ref[...]) * IDX_DIM ** -0.5).astype(o_ref.dtype)

    @pl.when(j == n_qk + n_v + n_qi - 1)
    def _():
        t = jnp.dot(h_ref[...], wt_ref[...], preferred_element_type=F32)
        lane = lax.broadcasted_iota(jnp.int32, t.shape, 1)
        tail_ref[...] = jnp.where(lane < IDX_DIM, _rope_idx(t, ci_ref[...], si_ref[...]), t * IDX_HEADS ** -0.5)


def _dsa_inproj(x, ln_g, mod3, w_main, w_tail, gain_row, tabs, n_heads, tm=1024, tn=512):
    b, s, d = x.shape
    dm = n_heads * HEAD_DIM
    n_qk, n_v, n_qi = 2 * dm // tn, dm // tn, IDX_HEADS * IDX_DIM // tn
    n_main = n_qk + n_v + n_qi
    tab_spec = pl.BlockSpec((None, tm, LANES), lambda b, i, j: (b, i, 0))
    return pl.pallas_call(
        functools.partial(_dsa_inproj_kernel, n_qk=n_qk, n_v=n_v, n_qi=n_qi),
        grid=(b, s // tm, n_main),
        in_specs=[
            pl.BlockSpec((None, tm, d), lambda b, i, j: (b, i, 0)),
            pl.BlockSpec((1, d), lambda b, i, j: (0, 0)),
            *_mod_specs(d, 0, 1),
            pl.BlockSpec((d, tn), lambda b, i, j: (0, j)),
            pl.BlockSpec((d, LANES), lambda b, i, j: (0, 0)),
            pl.BlockSpec((1, tn), lambda b, i, j: (0, j)),
            tab_spec, tab_spec, tab_spec, tab_spec,
        ],
        out_specs=[
            pl.BlockSpec((None, tm, tn), lambda b, i, j: (b, i, j)),
            pl.BlockSpec((None, tm, LANES), lambda b, i, j: (b, i, 0)),
        ],
        out_shape=[jax.ShapeDtypeStruct((b, s, n_main * tn), BF16),
                   jax.ShapeDtypeStruct((b, s, LANES), F32)],
        scratch_shapes=[pltpu.VMEM((tm, d), BF16)],
        compiler_params=_cparams(("parallel", "parallel", "arbitrary")),
        name="dsa_inproj",
    )(x, ln_g.reshape(1, d), mod3, mod3, w_main, w_tail, gain_row, *tabs)


def _dsa_core_kernel(q_ref, k_ref, v_ref, qi_ref, ki_ref, wi_ref, o_ref, key_ref, bias_ref, s_scr, p_scr,
                     *, t, topk, idx_bits, sub):
    i = pl.program_id(1)
    h = pl.program_id(2)
    nk = i + 1
    kf = float(topk)

    def chunk(c):
        return pl.ds(pl.multiple_of(c * t, t), t)

    @pl.when(h == 0)
    def _():
        qi = qi_ref[...]
        wi_t = wi_ref[...].T
        row = lax.broadcasted_iota(jnp.int32, (t, t), 0)
        col = lax.broadcasted_iota(jnp.int32, (t, t), 1)

        def score_body(c, _):
            ki = ki_ref[chunk(c), :][:, :IDX_DIM].astype(BF16)
            score = jnp.zeros((t, t), F32)
            for hh in range(IDX_HEADS):
                rel = lax.dot_general(ki, qi[:, hh * IDX_DIM:(hh + 1) * IDX_DIM], (((1,), (1,)), ((), ())),
                                      preferred_element_type=F32)
                score = score + wi_t[IDX_DIM + hh:IDX_DIM + hh + 1, :] * jnp.maximum(rel, 0.0)
            bits = pltpu.bitcast(score + 0.0, jnp.int32)
            key = jnp.where(bits < 0, bits ^ jnp.int32(0x7FFFFFFF), bits)
            key = jnp.where((c - i) * t + row <= col, key, jnp.int32(INT_MIN))
            key_ref[chunk(c), :] = key
            return 0

        lax.fori_loop(0, nk, score_body, 0)

        sub_row = lax.broadcasted_iota(jnp.int32, (sub, t), 0)

        def count(pred, *query_args):
            def body(c, acc):
                for r in range(t // sub):
                    start = pl.multiple_of(c * t + r * sub, sub)
                    hit = pred(key_ref[pl.ds(start, sub), :], start + sub_row, *query_args)
                    ones = jnp.where(hit, 1.0, 0.0)
                    acc = acc + jnp.sum(ones.reshape(sub // SUBLANES, SUBLANES, t), axis=0)
                return acc
            acc = lax.fori_loop(0, nk, body, jnp.zeros((SUBLANES, t), F32))
            return jnp.sum(acc, axis=0, keepdims=True)

        def bisect(n, thr):
            cand = thr + lax.shift_left(jnp.int32(1), 31 - n)
            return jnp.where(count(lambda key, pos, cd: key >= cd, cand) >= kf, cand, thr)

        thr = lax.fori_loop(0, 32, bisect, jnp.full((1, t), INT_MIN, jnp.int32))
        thr = jnp.maximum(thr, jnp.int32(INT_MIN + 1))
        n_ge = count(lambda key, pos, th: key >= th, thr)

        def tie_cut(_):
            need = kf - count(lambda key, pos, th: key > th, thr)
            def step(n, cut):
                cand = cut + lax.shift_left(jnp.int32(1), idx_bits - 1 - n)
                below = count(lambda key, pos, th, cd: (key == th) & (pos < cd), thr, cand)
                return jnp.where(below < need, cand, cut)
            return lax.fori_loop(0, idx_bits, step, jnp.zeros((1, t), jnp.int32))

        cut = lax.cond(jnp.max(n_ge) > kf, tie_cut, lambda _: jnp.full((1, t), 2 ** idx_bits - 1, jnp.int32), 0)

        def bias_body(c, _):
            key = key_ref[chunk(c), :]
            sel = (key > thr) | ((key == thr) & (c * t + row <= cut))
            bias_ref[chunk(c), :] = jnp.where(sel, 0.0, NEG_BIG)
            return 0

        lax.fori_loop(0, nk, bias_body, 0)

    q = q_ref[...]

    def logits(c):
        kc = k_ref[chunk(jnp.minimum(c, nk - 1)), :]
        return lax.dot_general(kc, q, (((1,), (1,)), ((), ())), preferred_element_type=F32)

    def weights(c, s, m, l):
        s = s + bias_ref[chunk(c), :]
        m_new = jnp.maximum(m, jnp.max(s, axis=0, keepdims=True))
        alpha = jnp.exp2(m - m_new)
        p = jnp.exp2(s - m_new)
        return p.astype(BF16), alpha, m_new, alpha * l + jnp.sum(p, axis=0, keepdims=True)

    def values(c, p, alpha, acc):
        pv = lax.dot_general(v_ref[chunk(c), :], p, (((0,), (0,)), ((), ())), preferred_element_type=F32)
        return alpha * acc + pv

    def step(n, st, slot):
        alpha, m, l, acc = st
        s_scr[1 - slot] = logits(n + 2)
        acc = values(n, p_scr[slot], alpha, acc)
        p, alpha, m, l = weights(n + 1, s_scr[slot], m, l)
        p_scr[1 - slot] = p
        return alpha, m, l, acc

    p0, alpha0, m0, l0 = weights(0, logits(0), jnp.full((1, t), NEG_BIG, F32), jnp.zeros((1, t), F32))
    p_scr[0] = p0
    s_scr[0] = logits(1)
    n_steps = nk - 1
    st = (alpha0, m0, l0, jnp.zeros((HEAD_DIM, t), F32))
    st = lax.fori_loop(0, lax.shift_right_logical(n_steps, 1),
                       lambda pp, c: step(2 * pp + 1, step(2 * pp, c, 0), 1), st)
    alpha, _, l, acc = lax.cond((n_steps & 1) == 1, lambda c: step(n_steps - 1, c, 0), lambda c: c, st)
    acc = values(nk - 1, p_scr[n_steps & 1], alpha, acc)
    o_ref[...] = (acc / l).T.astype(o_ref.dtype)


def _dsa_core(main, tail, n_heads, t=512):
    b, s, _ = main.shape
    dm = n_heads * HEAD_DIM
    topk = min(DSA_TOPK_MAX, s // 4)
    idx_bits = max(1, (s - 1).bit_length())
    qi_block = (3 * dm) // (IDX_HEADS * IDX_DIM)
    return pl.pallas_call(
        functools.partial(_dsa_core_kernel, t=t, topk=topk, idx_bits=idx_bits, sub=min(t, 64)),
        grid=(b, s // t, n_heads),
        in_specs=[
            pl.BlockSpec((None, t, HEAD_DIM), lambda b, i, h: (b, i, h)),
            pl.BlockSpec((None, s, HEAD_DIM), lambda b, i, h: (b, 0, n_heads + h)),
            pl.BlockSpec((None, s, HEAD_DIM), lambda b, i, h: (b, 0, 2 * n_heads + h)),
            pl.BlockSpec((None, t, IDX_HEADS * IDX_DIM), lambda b, i, h: (b, i, qi_block)),
            pl.BlockSpec((None, s, LANES), lambda b, i, h: (b, 0, 0)),
            pl.BlockSpec((None, t, LANES), lambda b, i, h: (b, i, 0)),
        ],
        out_specs=pl.BlockSpec((None, t, HEAD_DIM), lambda b, i, h: (b, i, h)),
        out_shape=jax.ShapeDtypeStruct((b, s, dm), BF16),
        scratch_shapes=[pltpu.VMEM((s, t), jnp.int32), pltpu.VMEM((s, t), F32),
                        pltpu.VMEM((2, t, t), F32), pltpu.VMEM((2, t, t), BF16)],
        compiler_params=_cparams(("parallel", "parallel", "arbitrary")),
        name="dsa_core",
    )(main, main, main, main, tail, tail)


def kernel(x, c, positions, ln1_g, ln2_g, ada_w, ada_b, mlp_w1, mlp_w2, sb_w_in, sb_q_gain, sb_k_gain, sb_w_out, s5_w_in, s5_lambda_re, s5_lambda_im, s5_log_dt, s5_b_re, s5_b_im, s5_c_re, s5_c_im, s5_d, s5_w_glu, dsa_w_in, dsa_q_gain, dsa_k_gain, dsa_w_out):
    depth = ada_w.shape[0]
    b, s, d = x.shape
    n_heads = d // HEAD_DIM
    scale = HEAD_DIM ** -0.5

    mod = _ada_mod(c, ada_w, ada_b)
    rope_tabs = _rope_tables(positions) if depth > 2 else None
    mlp_w1, mlp_w2, sb_w_in, sb_w_out, s5_w_in, s5_w_glu, dsa_w_out = (
        w.astype(BF16) for w in (mlp_w1, mlp_w2, sb_w_in, sb_w_out, s5_w_in, s5_w_glu, dsa_w_out))

    counts = [0, 0, 0]
    for i in range(depth):
        mod3 = mod[i].reshape(SUBLANES, 1, 6 * d)
        kind = i % N_MIXERS
        j = counts[kind]
        counts[kind] += 1
        if kind == 0:
            gain = jnp.concatenate([jnp.tile(sb_q_gain[j].astype(F32) * (scale * LOG2_E), n_heads),
                                    jnp.tile(sb_k_gain[j].astype(F32), n_heads),
                                    jnp.ones((d,), F32)]).reshape(1, 3 * d)
            qkv = _sb_inproj(x, ln1_g[i], mod3, sb_w_in, j, gain)
            o = _sb_attention(qkv, n_heads)
            x = _out_gate(o, sb_w_out, j, x, mod3, 2)
        elif kind == 1:
            u = _nm_matmul(x, ln1_g[i], mod3, s5_w_in, j, F32)
            tables = _s5_tables(s5_lambda_re[j], s5_lambda_im[j], s5_log_dt[j], s5_b_re[j], s5_b_im[j],
                                s5_c_re[j], s5_c_im[j])
            z = _s5_core(u, tables, s5_d[j])
            x = _glu_gate(z, s5_w_glu, j, x, mod3, 2)
        else:
            tn = 512
            n_in = dsa_w_in.shape[2]
            n_main = 3 * d + IDX_HEADS * IDX_DIM
            w_in = dsa_w_in[j].astype(BF16)
            w_tail = jnp.zeros((d, LANES), BF16).at[:, :n_in - n_main].set(w_in[:, n_main:])
            gain = jnp.zeros((1, n_main), F32)
            gain = gain.at[0, :d].set(jnp.tile(dsa_q_gain[j].astype(F32) * (scale * LOG2_E), n_heads))
            gain = gain.at[0, d:2 * d].set(jnp.tile(dsa_k_gain[j].astype(F32), n_heads))
            main, tail = _dsa_inproj(x, ln1_g[i], mod3, w_in[:, :n_main], w_tail, gain, rope_tabs, n_heads, tn=tn)
            o = _dsa_core(main, tail, n_heads)
            x = _out_gate(o, dsa_w_out, j, x, mod3, 2)
        x = _mlp(x, ln2_g[i], mod3, mlp_w1, mlp_w2, i)
    return x
```

```python
import functools
import math

import jax
import jax.numpy as jnp
from jax import lax
from jax.experimental import pallas as pl
from jax.experimental.pallas import tpu as pltpu

F32 = jnp.float32
BF16 = jnp.bfloat16

N_MIXERS = 3
HEAD_DIM = 128
ROPE_THETA = 10000.0
EPS = 1e-6
S5_GROUP = 16
S5_STATE = 64
IDX_HEADS = 16
IDX_DIM = 64
DSA_TOPK_MAX = 256

LANES = 128
SUBLANES = 8
ROW_STRIP = 16
VMEM_LIMIT_BYTES = 56 * 1024 * 1024

S5_CHUNK = 16
NEG_BIG = -1e30
INT_MIN = -(2 ** 31)
LOG2_E = 1.4426950408889634


def _cparams(sem):
    return pltpu.CompilerParams(dimension_semantics=sem, vmem_limit_bytes=VMEM_LIMIT_BYTES)


def _sigmoid(x):
    return 1.0 / (1.0 + jnp.exp(-x))


def _gelu_tanh(x):
    c = math.sqrt(2.0 / math.pi)
    return 0.5 * x * (1.0 + jnp.tanh(c * (x + 0.044715 * (x * x * x))))


def _rms_mod_store(h_ref, x_ref, g_ref, sh_ref, sc_ref):
    d = x_ref.shape[1]
    gain = jnp.broadcast_to(g_ref[...] * (1.0 + sc_ref[...]), (ROW_STRIP, d))
    shift = jnp.broadcast_to(sh_ref[...], (ROW_STRIP, d))

    def body(r, _):
        rows = pl.ds(pl.multiple_of(r * ROW_STRIP, ROW_STRIP), ROW_STRIP)
        x = x_ref[rows, :]
        ms = jnp.mean(x * x, axis=-1, keepdims=True)
        h_ref[rows, :] = (x * lax.rsqrt(ms + EPS) * gain + shift).astype(h_ref.dtype)
        return 0

    lax.fori_loop(0, x_ref.shape[0] // ROW_STRIP, body, 0, unroll=8)


def _head_rmsnorm(y, gain):
    ms = jnp.mean(y * y, axis=-1, keepdims=True)
    return y * lax.rsqrt(ms + EPS) * gain


def _ada_kernel(c_ref, w_ref, b_ref, o_ref):
    c = c_ref[...]
    cond = (c * _sigmoid(c)).astype(BF16)
    o_ref[...] = jnp.dot(cond, w_ref[...].astype(BF16), preferred_element_type=F32) + b_ref[...]


def _ada_mod(c, ada_w, ada_b):
    depth, d, n = ada_w.shape
    b = c.shape[0]
    c8 = jnp.zeros((SUBLANES, d), F32).at[:b].set(c)
    tn = 1024
    return pl.pallas_call(
        _ada_kernel,
        grid=(depth, n // tn),
        in_specs=[
            pl.BlockSpec((SUBLANES, d), lambda l, j: (0, 0)),
            pl.BlockSpec((None, d, tn), lambda l, j: (l, 0, j)),
            pl.BlockSpec((None, 1, tn), lambda l, j: (l, 0, j)),
        ],
        out_specs=pl.BlockSpec((None, SUBLANES, tn), lambda l, j: (l, 0, j)),
        out_shape=jax.ShapeDtypeStruct((depth, SUBLANES, n), F32),
        compiler_params=_cparams(("parallel", "parallel")),
        name="ada_mod",
    )(c8, ada_w, ada_b.reshape(depth, 1, n))


def _mod_specs(d, shift_chunk, scale_chunk):
    return [
        pl.BlockSpec((None, 1, d), lambda b, i, j: (b, 0, shift_chunk)),
        pl.BlockSpec((None, 1, d), lambda b, i, j: (b, 0, scale_chunk)),
    ]


def _nm_matmul_kernel(x_ref, g_ref, sh_ref, sc_ref, w_ref, o_ref, h_ref):
    @pl.when(pl.program_id(2) == 0)
    def _():
        _rms_mod_store(h_ref, x_ref, g_ref, sh_ref, sc_ref)

    o_ref[...] = jnp.dot(h_ref[...], w_ref[...], preferred_element_type=F32).astype(o_ref.dtype)


def _nm_matmul(x, ln_g, mod3, w, layer, out_dtype, tm=1024, tn=1024):
    b, s, d = x.shape
    n = w.shape[2]
    return pl.pallas_call(
        _nm_matmul_kernel,
        grid=(b, s // tm, n // tn),
        in_specs=[
            pl.BlockSpec((None, tm, d), lambda b, i, j: (b, i, 0)),
            pl.BlockSpec((1, d), lambda b, i, j: (0, 0)),
            *_mod_specs(d, 0, 1),
            pl.BlockSpec((None, d, tn), lambda b, i, j: (layer, 0, j)),
        ],
        out_specs=pl.BlockSpec((None, tm, tn), lambda b, i, j: (b, i, j)),
        out_shape=jax.ShapeDtypeStruct((b, s, n), out_dtype),
        scratch_shapes=[pltpu.VMEM((tm, d), BF16)],
        compiler_params=_cparams(("parallel", "parallel", "arbitrary")),
        name="nm_matmul",
    )(x, ln_g.reshape(1, d), mod3, mod3, w)


def _sb_inproj_kernel(x_ref, g_ref, sh_ref, sc_ref, w_ref, gain_ref, o_ref, h_ref, *, n_norm_tiles):
    j = pl.program_id(2)

    @pl.when(j == 0)
    def _():
        _rms_mod_store(h_ref, x_ref, g_ref, sh_ref, sc_ref)

    y = jnp.dot(h_ref[...], w_ref[...], preferred_element_type=F32)

    @pl.when(j < n_norm_tiles)
    def _():
        for hh in range(y.shape[1] // HEAD_DIM):
            sl = slice(hh * HEAD_DIM, (hh + 1) * HEAD_DIM)
            o_ref[:, sl] = _head_rmsnorm(y[:, sl], gain_ref[:, sl]).astype(o_ref.dtype)

    @pl.when(j >= n_norm_tiles)
    def _():
        o_ref[...] = y.astype(o_ref.dtype)


def _sb_inproj(x, ln_g, mod3, w, layer, gain_row, tm=1024, tn=1024):
    b, s, d = x.shape
    n = w.shape[2]
    return pl.pallas_call(
        functools.partial(_sb_inproj_kernel, n_norm_tiles=2 * d // tn),
        grid=(b, s // tm, n // tn),
        in_specs=[
            pl.BlockSpec((None, tm, d), lambda b, i, j: (b, i, 0)),
            pl.BlockSpec((1, d), lambda b, i, j: (0, 0)),
            *_mod_specs(d, 0, 1),
            pl.BlockSpec((None, d, tn), lambda b, i, j: (layer, 0, j)),
            pl.BlockSpec((1, tn), lambda b, i, j: (0, j)),
        ],
        out_specs=pl.BlockSpec((None, tm, tn), lambda b, i, j: (b, i, j)),
        out_shape=jax.ShapeDtypeStruct((b, s, n), BF16),
        scratch_shapes=[pltpu.VMEM((tm, d), BF16)],
        compiler_params=_cparams(("parallel", "parallel", "arbitrary")),
        name="sb_inproj",
    )(x, ln_g.reshape(1, d), mod3, mod3, w, gain_row)


def _out_gate_kernel(a_ref, w_ref, x_ref, gate_ref, o_ref):
    y = jnp.dot(a_ref[...], w_ref[...], preferred_element_type=F32)
    o_ref[...] = x_ref[...] + gate_ref[...] * y


def _out_gate(a, w, layer, x, mod3, gate_chunk, tm=1024, tn=1024):
    b, s, d = x.shape
    k = a.shape[2]
    nt = d // tn
    return pl.pallas_call(
        _out_gate_kernel,
        grid=(b, s // tm, nt),
        in_specs=[
            pl.BlockSpec((None, tm, k), lambda b, i, j: (b, i, 0)),
            pl.BlockSpec((None, k, tn), lambda b, i, j: (layer, 0, j)),
            pl.BlockSpec((None, tm, tn), lambda b, i, j: (b, i, j)),
            pl.BlockSpec((None, 1, tn), lambda b, i, j: (b, 0, gate_chunk * nt + j)),
        ],
        out_specs=pl.BlockSpec((None, tm, tn), lambda b, i, j: (b, i, j)),
        out_shape=jax.ShapeDtypeStruct((b, s, d), F32),
        compiler_params=_cparams(("parallel", "parallel", "parallel")),
        name="out_gate",
    )(a, w, x, mod3)


def _glu_gate_kernel(z_ref, wa_ref, wg_ref, x_ref, gate_ref, o_ref, zb_ref):
    @pl.when(pl.program_id(2) == 0)
    def _():
        zb_ref[...] = z_ref[...].astype(BF16)

    z = zb_ref[...]
    a = jnp.dot(z, wa_ref[...], preferred_element_type=F32)
    g = jnp.dot(z, wg_ref[...], preferred_element_type=F32)
    o_ref[...] = x_ref[...] + gate_ref[...] * (a * _sigmoid(g))


def _glu_gate(z, w_glu, layer, x, mod3, gate_chunk, tm=1024, tn=512):
    b, s, d = x.shape
    nt = d // tn
    return pl.pallas_call(
        _glu_gate_kernel,
        grid=(b, s // tm, nt),
        in_specs=[
            pl.BlockSpec((None, tm, d), lambda b, i, j: (b, i, 0)),
            pl.BlockSpec((None, d, tn), lambda b, i, j: (layer, 0, j)),
            pl.BlockSpec((None, d, tn), lambda b, i, j: (layer, 0, nt + j)),
            pl.BlockSpec((None, tm, tn), lambda b, i, j: (b, i, j)),
            pl.BlockSpec((None, 1, tn), lambda b, i, j: (b, 0, gate_chunk * nt + j)),
        ],
        out_specs=pl.BlockSpec((None, tm, tn), lambda b, i, j: (b, i, j)),
        out_shape=jax.ShapeDtypeStruct((b, s, d), F32),
        scratch_shapes=[pltpu.VMEM((tm, d), BF16)],
        compiler_params=_cparams(("parallel", "parallel", "arbitrary")),
        name="glu_gate",
    )(z, w_glu, w_glu, x, mod3)


def _mlp_kernel(x_ref, g_ref, sh_ref, sc_ref, w1_ref, w2_ref, gate_ref, o_ref, h_ref, acc_ref):
    j = pl.program_id(2)

    @pl.when(j == 0)
    def _():
        _rms_mod_store(h_ref, x_ref, g_ref, sh_ref, sc_ref)
        acc_ref[...] = jnp.zeros_like(acc_ref)

    a = jnp.maximum(jnp.dot(h_ref[...], w1_ref[...], preferred_element_type=F32), 0.0)
    acc_ref[...] += jnp.dot((a * a).astype(BF16), w2_ref[...], preferred_element_type=F32)

    @pl.when(j == pl.num_programs(2) - 1)
    def _():
        o_ref[...] = x_ref[...] + gate_ref[...] * acc_ref[...]


def _mlp(x, ln_g, mod3, w1, w2, layer, tm=512, tf=1024):
    b, s, d = x.shape
    f = w1.shape[2]
    return pl.pallas_call(
        _mlp_kernel,
        grid=(b, s // tm, f // tf),
        in_specs=[
            pl.BlockSpec((None, tm, d), lambda b, i, j: (b, i, 0)),
            pl.BlockSpec((1, d), lambda b, i, j: (0, 0)),
            *_mod_specs(d, 3, 4),
            pl.BlockSpec((None, d, tf), lambda b, i, j: (layer, 0, j)),
            pl.BlockSpec((None, tf, d), lambda b, i, j: (layer, j, 0)),
            pl.BlockSpec((None, 1, d), lambda b, i, j: (b, 0, 5)),
        ],
        out_specs=pl.BlockSpec((None, tm, d), lambda b, i, j: (b, i, 0)),
        out_shape=jax.ShapeDtypeStruct((b, s, d), F32),
        scratch_shapes=[pltpu.VMEM((tm, d), BF16), pltpu.VMEM((tm, d), F32)],
        compiler_params=_cparams(("parallel", "parallel", "arbitrary")),
        name="mlp",
    )(x, ln_g.reshape(1, d), mod3, mod3, w1, w2, mod3)


def _sb_attn_kernel(q_ref, k_ref, v_ref, later_ref, o_ref, z_scr, e_scr, *, tq, tk, hps):
    i = pl.program_id(2)
    m = (i + 1) * (tq // tk)
    n_diag = tq // tk
    heads = range(hps)
    cols = [slice(hh * HEAD_DIM, (hh + 1) * HEAD_DIM) for hh in heads]
    qs = [q_ref[:, cols[hh]] for hh in heads]
    lead = lax.broadcasted_iota(jnp.int32, (tq, tk), 0) - lax.broadcasted_iota(jnp.int32, (tq, tk), 1)
    sign = jnp.uint32(0x80000000)

    def key_start(n):
        return pl.multiple_of(jnp.maximum(m - 1 - n, 0) * tk, tk)

    def score(hh, n):
        kb = k_ref[pl.ds(key_start(n), tk), cols[hh]]
        return lax.dot_general(qs[hh], kb, (((1,), (1,)), ((), ())), preferred_element_type=F32)

    def log_weights(n, z, masked, lead=lead):
        neg_abs = pltpu.bitcast(pltpu.bitcast(z, jnp.uint32) | sign, F32)
        ls = jnp.minimum(z, 0.0) - jnp.log(1.0 + jnp.exp2(neg_abs)) * LOG2_E
        ln = ls - z
        if masked:
            causal = key_start(n) - i * tq < lead
            ln = jnp.where(causal, ln, 0.0)
            ls = jnp.where(causal, ls, NEG_BIG)
        ln = ln.astype(BF16)
        later = jnp.dot(ln, later_ref[...], preferred_element_type=F32)
        return ls + later, later[:, 0:1] + ln[:, 0:1].astype(F32)

    def accumulate(hh, n, e, carry, acc):
        vb = v_ref[pl.ds(key_start(n), tk), cols[hh]]
        w = jnp.exp2(e + carry).astype(BF16)
        return acc + jnp.dot(w, vb, preferred_element_type=F32)

    states = []
    top = tq - tk
    for hh in heads:
        kb = k_ref[pl.ds(key_start(0), tk), cols[hh]]
        z0 = lax.dot_general(qs[hh][top:], kb, (((1,), (1,)), ((), ())), preferred_element_type=F32)
        e0, tot0 = log_weights(0, z0, True, lead[top:])
        e0 = jnp.concatenate([jnp.full((top, tk), NEG_BIG, F32), e0], axis=0)
        tot0 = jnp.concatenate([jnp.zeros((top, 1), F32), tot0], axis=0)
        e_scr[hh, 0] = e0
        z_scr[hh, 0] = score(hh, 1)
        states.append((jnp.zeros((tq, 1), F32), tot0, jnp.zeros((tq, HEAD_DIM), F32)))

    def step(n, sts):
        slot = n % 2
        out = []
        for hh in heads:
            carry, tot, acc = sts[hh]
            z_scr[hh, 1 - slot] = score(hh, n + 2)
            acc = accumulate(hh, n, e_scr[hh, slot], carry, acc)
            e_next, tot_next = log_weights(n + 1, z_scr[hh, slot], True)
            e_scr[hh, 1 - slot] = e_next
            out.append((carry + tot, tot_next, acc))
        return out

    def pair(p, sts):
        n = n_diag - 1 + 2 * p
        z_mid = [score(hh, n + 2) for hh in heads]
        accs = [accumulate(hh, n, e_scr[hh, 1], sts[hh][0], sts[hh][2]) for hh in heads]
        mids = [log_weights(n + 1, z_scr[hh, 1], False) for hh in heads]
        carries = [sts[hh][0] + sts[hh][1] for hh in heads]
        for hh in heads:
            z_scr[hh, 1] = score(hh, n + 3)
        accs = [accumulate(hh, n + 1, mids[hh][0], carries[hh], accs[hh]) for hh in heads]
        lasts = [log_weights(n + 2, z_mid[hh], False) for hh in heads]
        for hh in heads:
            e_scr[hh, 1] = lasts[hh][0]
        return tuple((carries[hh] + mids[hh][1], lasts[hh][1], accs[hh]) for hh in heads)

    assert n_diag % 2 == 0
    for n in range(n_diag - 1):
        states = step(n, states)
    n_pairs = (i * n_diag) // 2
    n_quads = lax.shift_right_logical(n_pairs, 1)
    states = lax.fori_loop(0, n_quads, lambda qd, c: pair(2 * qd + 1, pair(2 * qd, c)), tuple(states))
    states = lax.fori_loop(2 * n_quads, n_pairs, pair, states)
    for hh in heads:
        carry, _, acc = states[hh]
        o_ref[:, cols[hh]] = accumulate(hh, m - 1, e_scr[hh, 1], carry, acc).astype(o_ref.dtype)


def _sb_attention(qkv, n_heads, tq=512, tk=256, hps=1):
    b, s, _ = qkv.shape
    groups = n_heads // hps
    width = hps * HEAD_DIM
    pos = jnp.arange(tk)
    later_mat = (pos[:, None] > pos[None, :]).astype(BF16)
    return pl.pallas_call(
        functools.partial(_sb_attn_kernel, tq=tq, tk=tk, hps=hps),
        grid=(b, groups, s // tq),
        in_specs=[
            pl.BlockSpec((None, tq, width), lambda b, g, i: (b, i, g)),
            pl.BlockSpec((None, s, width), lambda b, g, i: (b, 0, groups + g)),
            pl.BlockSpec((None, s, width), lambda b, g, i: (b, 0, 2 * groups + g)),
            pl.BlockSpec((tk, tk), lambda b, g, i: (0, 0)),
        ],
        out_specs=pl.BlockSpec((None, tq, width), lambda b, g, i: (b, i, g)),
        out_shape=jax.ShapeDtypeStruct((b, s, n_heads * HEAD_DIM), BF16),
        scratch_shapes=[pltpu.VMEM((hps, 2, tq, tk), F32), pltpu.VMEM((hps, 2, tq, tk), F32)],
        compiler_params=_cparams(("parallel", "parallel", "parallel")),
        name="sb_attn",
    )(qkv, qkv, qkv, later_mat)


def _s5_tables(lam_re, lam_im, log_dt, b_re, b_im, c_re, c_im):
    g, p, gc = b_re.shape
    l = S5_CHUNK
    gpt = LANES // gc
    nt = g // gpt
    hi = lax.Precision.HIGHEST
    dt = jnp.exp(log_dt.astype(F32))[:, None]
    lr = lam_re.astype(F32)
    li = lam_im.astype(F32)
    mag = jnp.exp(lr * dt)
    ar = mag * jnp.cos(li * dt)
    ai = mag * jnp.sin(li * dt)
    den = lr * lr + li * li
    fr = ((ar - 1.0) * lr + ai * li) / den
    fi = (ai * lr - (ar - 1.0) * li) / den
    br_ = b_re.astype(F32)
    bi_ = b_im.astype(F32)
    bbr = fr[..., None] * br_ - fi[..., None] * bi_
    bbi = fr[..., None] * bi_ + fi[..., None] * br_
    cr = c_re.astype(F32)
    ci = c_im.astype(F32)
    n = jnp.arange(l + 1, dtype=F32)[:, None, None]
    pw_r = jnp.exp(n * (lr * dt)) * jnp.cos(n * (li * dt))
    pw_i = jnp.exp(n * (lr * dt)) * jnp.sin(n * (li * dt))
    bt_r = bbr.transpose(0, 2, 1)
    bt_i = bbi.transpose(0, 2, 1)
    ab_r = pw_r[:l, :, None, :] * bt_r - pw_i[:l, :, None, :] * bt_i
    ab_i = pw_r[:l, :, None, :] * bt_i + pw_i[:l, :, None, :] * bt_r

    def tile_rows(t):
        cols = t.shape[-1]
        return t.reshape(l, nt, gpt, gc, cols).transpose(1, 0, 2, 3, 4).reshape(nt, l * LANES, cols)

    wz = tile_rows(jnp.concatenate([ab_r[::-1], ab_i[::-1]], axis=-1))

    kern = (jnp.einsum('gcp,ngdp->gdnc', cr, ab_r, precision=hi)
            - jnp.einsum('gcp,ngdp->gdnc', ci, ab_i, precision=hi)).reshape(g, gc, l * gc)
    intra = tile_rows(jnp.stack([jnp.pad(kern[:, :, :(l - j) * gc], ((0, 0), (0, 0), (j * gc, 0)))
                                 for j in range(l)]))

    ct_r = jnp.tile(cr.transpose(0, 2, 1), (1, 1, l))
    ct_i = jnp.tile(ci.transpose(0, 2, 1), (1, 1, l))
    p1_r = jnp.repeat(pw_r[1:].transpose(1, 2, 0), gc, axis=-1)
    p1_i = jnp.repeat(pw_i[1:].transpose(1, 2, 0), gc, axis=-1)
    vr = (ct_r * p1_r - ct_i * p1_i).reshape(nt, gpt * p, l * gc)
    vi = (-(ct_r * p1_i + ct_i * p1_r)).reshape(nt, gpt * p, l * gc)
    wy = jnp.concatenate([intra, vr, vi], axis=1)

    alr = pw_r[l].reshape(nt, 1, gpt * p)
    ali = pw_i[l].reshape(nt, 1, gpt * p)
    return wz.astype(BF16), wy.astype(BF16), alr, ali


def _spread_groups(src_ref, dst_ref, row_group_div, col_unit):
    n_src = src_ref.shape[1]
    n_dst = dst_ref.shape[1]
    groups = n_dst // n_src
    rc = 256
    sr = lax.broadcasted_iota(jnp.int32, (n_src, n_dst), 0)
    dc = lax.broadcasted_iota(jnp.int32, (n_src, n_dst), 1)
    spread = ((sr // col_unit == dc // (col_unit * groups)) & (sr % col_unit == dc % col_unit))
    spread = jnp.where(spread, 1.0, 0.0).astype(BF16)
    rr = lax.broadcasted_iota(jnp.int32, (rc, n_dst), 0)
    cg = (lax.broadcasted_iota(jnp.int32, (rc, n_dst), 1) // col_unit) % groups
    for r0 in range(0, src_ref.shape[0], rc):
        wide = jnp.dot(src_ref[r0:r0 + rc, :], spread, preferred_element_type=F32)
        keep = ((r0 + rr) // row_group_div) % groups == cg
        dst_ref[r0:r0 + rc, :] = jnp.where(keep, wide, 0.0).astype(dst_ref.dtype)


def _s5_core_kernel(ut_ref, wzc_ref, wyc_ref, alr_ref, ali_ref, d_ref, o_ref,
                    wz_ref, wy_ref, zr_ref, zi_ref, hr_ref, hi_ref, *, l, nk, gc, p):
    @pl.when(pl.program_id(1) == 0)
    def _():
        n_in = l * LANES
        _spread_groups(wzc_ref, wz_ref, gc, p)
        _spread_groups(wyc_ref.at[:n_in], wy_ref.at[:n_in], gc, gc)
        _spread_groups(wyc_ref.at[n_in:], wy_ref.at[n_in:], p, gc)

    us = [ut_ref[pl.ds(j, nk, stride=l), :] for j in range(l)]
    ub = jnp.concatenate([u.astype(BF16) for u in us], axis=1)
    z = jnp.dot(ub, wz_ref[...], preferred_element_type=F32)
    half = z.shape[1] // 2
    zr_ref[...] = z[:, :half]
    zi_ref[...] = z[:, half:]
    alr = alr_ref[...]
    ali = ali_ref[...]
    rows = lax.broadcasted_iota(jnp.int32, (SUBLANES, half), 0)

    def body(kb, c):
        hr, hi = c
        base = pl.multiple_of(kb * SUBLANES, SUBLANES)
        zr = zr_ref[pl.ds(base, SUBLANES), :]
        zi = zi_ref[pl.ds(base, SUBLANES), :]
        out_r = jnp.zeros((SUBLANES, half), F32)
        out_i = jnp.zeros((SUBLANES, half), F32)
        for r in range(SUBLANES):
            out_r = jnp.where(rows == r, hr, out_r)
            out_i = jnp.where(rows == r, hi, out_i)
            hr, hi = (alr * hr - ali * hi + zr[r:r + 1, :], alr * hi + ali * hr + zi[r:r + 1, :])
        hr_ref[pl.ds(base, SUBLANES), :] = out_r
        hi_ref[pl.ds(base, SUBLANES), :] = out_i
        return hr, hi

    zero = jnp.zeros((1, half), F32)
    lax.fori_loop(0, nk // SUBLANES, body, (zero, zero))

    lhs = jnp.concatenate([ub, hr_ref[...].astype(BF16), hi_ref[...].astype(BF16)], axis=1)
    y = jnp.dot(lhs, wy_ref[...], preferred_element_type=F32)
    d = d_ref[...]
    for i in range(l):
        yi = y[:, i * LANES:(i + 1) * LANES] + d * us[i]
        o_ref[pl.ds(i, nk, stride=l), :] = _gelu_tanh(yi).astype(o_ref.dtype)


def _s5_core(u, tables, d_skip):
    b, s, d = u.shape
    l = S5_CHUNK
    nk = s // l
    nt = d // LANES
    wz, wy, alr, ali = tables
    half = alr.shape[2]
    gpt = LANES // S5_GROUP
    return pl.pallas_call(
        functools.partial(_s5_core_kernel, l=l, nk=nk, gc=S5_GROUP, p=half // gpt),
        grid=(nt, b),
        in_specs=[
            pl.BlockSpec((None, s, LANES), lambda t, b: (b, 0, t)),
            pl.BlockSpec((None,) + wz.shape[1:], lambda t, b: (t, 0, 0)),
            pl.BlockSpec((None,) + wy.shape[1:], lambda t, b: (t, 0, 0)),
            pl.BlockSpec((None, 1, half), lambda t, b: (t, 0, 0)),
            pl.BlockSpec((None, 1, half), lambda t, b: (t, 0, 0)),
            pl.BlockSpec((None, 1, LANES), lambda t, b: (t, 0, 0)),
        ],
        out_specs=pl.BlockSpec((None, s, LANES), lambda t, b: (b, 0, t)),
        out_shape=jax.ShapeDtypeStruct((b, s, d), F32),
        scratch_shapes=[pltpu.VMEM((l * LANES, 2 * half), BF16),
                        pltpu.VMEM((l * LANES + 2 * half, l * LANES), BF16)] + [pltpu.VMEM((nk, half), F32)] * 4,
        compiler_params=_cparams(("arbitrary", "arbitrary")),
        name="s5_core",
    )(u, wz, wy, alr, ali, d_skip.astype(F32).reshape(nt, 1, LANES))


def _rope_tab_kernel(pos_ref, f_head_ref, f_idx_ref, ch_ref, sh_ref, ci_ref, si_ref):
    p = pos_ref[...].astype(F32)
    lane = lax.broadcasted_iota(jnp.int32, ch_ref.shape, 1)
    a = p * f_head_ref[...]
    ch_ref[...] = jnp.cos(a)
    sh_ref[...] = jnp.where(lane < HEAD_DIM // 2, -jnp.sin(a), jnp.sin(a))
    a = p * f_idx_ref[...]
    ci_ref[...] = jnp.cos(a)
    si_ref[...] = jnp.where(lane % IDX_DIM < IDX_DIM // 2, -jnp.sin(a), jnp.sin(a))


def _rope_tables(positions, tm=1024):
    b, s = positions.shape

    def inv_freq(dim):
        return ROPE_THETA ** (-jnp.arange(0, dim, 2, dtype=F32) / dim)

    f_head = jnp.tile(inv_freq(HEAD_DIM), 2).reshape(1, LANES)
    f_idx = jnp.tile(inv_freq(IDX_DIM), 2 * LANES // IDX_DIM).reshape(1, LANES)
    row_spec = pl.BlockSpec((None, tm, LANES), lambda b, i: (b, i, 0))
    return pl.pallas_call(
        _rope_tab_kernel,
        grid=(b, s // tm),
        in_specs=[
            pl.BlockSpec((None, tm, 1), lambda b, i: (b, i, 0)),
            pl.BlockSpec((1, LANES), lambda b, i: (0, 0)),
            pl.BlockSpec((1, LANES), lambda b, i: (0, 0)),
        ],
        out_specs=[row_spec] * 4,
        out_shape=[jax.ShapeDtypeStruct((b, s, LANES), F32)] * 4,
        compiler_params=_cparams(("parallel", "parallel")),
        name="rope_tables",
    )(positions.reshape(b, s, 1), f_head, f_idx)


def _rope_head(y, cos, sin_signed):
    return y * cos + pltpu.roll(y, HEAD_DIM // 2, axis=1) * sin_signed


def _rope_idx(y, cos, sin_signed):
    lane = lax.broadcasted_iota(jnp.int32, y.shape, 1)
    half = IDX_DIM // 2
    partner = jnp.where(lane % IDX_DIM < half, pltpu.roll(y, LANES - half, axis=1), pltpu.roll(y, half, axis=1))
    return y * cos + partner * sin_signed


def _dsa_inproj_kernel(x_ref, g_ref, sh_ref, sc_ref, w_ref, wt_ref, gain_ref, ch_ref, shd_ref, ci_ref, si_ref,
                       o_ref, tail_ref, h_ref, *, n_qk, n_v, n_qi):
    j = pl.program_id(2)

    @pl.when(j == 0)
    def _():
        _rms_mod_store(h_ref, x_ref, g_ref, sh_ref, sc_ref)

    y = jnp.dot(h_ref[...], w_ref[...], preferred_element_type=F32)
    slices = [slice(hh * LANES, (hh + 1) * LANES) for hh in range(y.shape[1] // LANES)]

    @pl.when(j < n_qk)
    def _():
        for sl in slices:
            yn = _head_rmsnorm(y[:, sl], gain_ref[:, sl])
            o_ref[:, sl] = _rope_head(yn, ch_ref[...], shd_ref[...]).astype(o_ref.dtype)

    @pl.when((j >= n_qk) & (j < n_qk + n_v))
    def _():
        o_ref[...] = y.astype(o_ref.dtype)

    @pl.when((j >= n_qk + n_v) & (j < n_qk + n_v + n_qi))
    def _():
        for sl in slices:
            o_ref[:, sl] = (_rope_idx(y[:, sl], ci_ref[...], si_ref[...]) * IDX_DIM ** -0.5).astype(o_ref.dtype)

    @pl.when(j == n_qk + n_v + n_qi - 1)
    def _():
        t = jnp.dot(h_ref[...], wt_ref[...], preferred_element_type=F32)
        lane = lax.broadcasted_iota(jnp.int32, t.shape, 1)
        tail_ref[...] = jnp.where(lane < IDX_DIM, _rope_idx(t, ci_ref[...], si_ref[...]), t * IDX_HEADS ** -0.5)


def _dsa_inproj(x, ln_g, mod3, w_main, w_tail, gain_row, tabs, n_heads, tm=1024, tn=512):
    b, s, d = x.shape
    dm = n_heads * HEAD_DIM
    n_qk, n_v, n_qi = 2 * dm // tn, dm // tn, IDX_HEADS * IDX_DIM // tn
    n_main = n_qk + n_v + n_qi
    tab_spec = pl.BlockSpec((None, tm, LANES), lambda b, i, j: (b, i, 0))
    return pl.pallas_call(
        functools.partial(_dsa_inproj_kernel, n_qk=n_qk, n_v=n_v, n_qi=n_qi),
        grid=(b, s // tm, n_main),
        in_specs=[
            pl.BlockSpec((None, tm, d), lambda b, i, j: (b, i, 0)),
            pl.BlockSpec((1, d), lambda b, i, j: (0, 0)),
            *_mod_specs(d, 0, 1),
            pl.BlockSpec((d, tn), lambda b, i, j: (0, j)),
            pl.BlockSpec((d, LANES), lambda b, i, j: (0, 0)),
            pl.BlockSpec((1, tn), lambda b, i, j: (0, j)),
            tab_spec, tab_spec, tab_spec, tab_spec,
        ],
        out_specs=[
            pl.BlockSpec((None, tm, tn), lambda b, i, j: (b, i, j)),
            pl.BlockSpec((None, tm, LANES), lambda b, i, j: (b, i, 0)),
        ],
        out_shape=[jax.ShapeDtypeStruct((b, s, n_main * tn), BF16),
                   jax.ShapeDtypeStruct((b, s, LANES), F32)],
        scratch_shapes=[pltpu.VMEM((tm, d), BF16)],
        compiler_params=_cparams(("parallel", "parallel", "arbitrary")),
        name="dsa_inproj",
    )(x, ln_g.reshape(1, d), mod3, mod3, w_main, w_tail, gain_row, *tabs)


def _dsa_core_kernel(q_ref, k_ref, v_ref, qi_ref, ki_ref, wi_ref, o_ref, key_ref, bias_ref, s_scr, p_scr,
                     *, t, topk, idx_bits, sub):
    i = pl.program_id(1)
    h = pl.program_id(2)
    nk = i + 1
    kf = float(topk)

    def chunk(c):
        return pl.ds(pl.multiple_of(c * t, t), t)

    @pl.when(h == 0)
    def _():
        qi = qi_ref[...]
        wi_t = wi_ref[...].T
        row = lax.broadcasted_iota(jnp.int32, (t, t), 0)
        col = lax.broadcasted_iota(jnp.int32, (t, t), 1)

        def score_body(c, _):
            ki = ki_ref[chunk(c), :][:, :IDX_DIM].astype(BF16)
            score = jnp.zeros((t, t), F32)
            for hh in range(IDX_HEADS):
                rel = lax.dot_general(ki, qi[:, hh * IDX_DIM:(hh + 1) * IDX_DIM], (((1,), (1,)), ((), ())),
                                      preferred_element_type=F32)
                score = score + wi_t[IDX_DIM + hh:IDX_DIM + hh + 1, :] * jnp.maximum(rel, 0.0)
            bits = pltpu.bitcast(score + 0.0, jnp.int32)
            key = jnp.where(bits < 0, bits ^ jnp.int32(0x7FFFFFFF), bits)
            key = jnp.where((c - i) * t + row <= col, key, jnp.int32(INT_MIN))
            key_ref[chunk(c), :] = key
            return 0

        lax.fori_loop(0, nk, score_body, 0)

        sub_row = lax.broadcasted_iota(jnp.int32, (sub, t), 0)

        def count(pred, *query_args):
            def body(c, acc):
                for r in range(t // sub):
                    start = pl.multiple_of(c * t + r * sub, sub)
                    hit = pred(key_ref[pl.ds(start, sub), :], start + sub_row, *query_args)
                    ones = jnp.where(hit, 1.0, 0.0)
                    acc = acc + jnp.sum(ones.reshape(sub // SUBLANES, SUBLANES, t), axis=0)
                return acc
            acc = lax.fori_loop(0, nk, body, jnp.zeros((SUBLANES, t), F32))
            return jnp.sum(acc, axis=0, keepdims=True)

        def bisect(n, thr):
            cand = thr + lax.shift_left(jnp.int32(1), 31 - n)
            return jnp.where(count(lambda key, pos, cd: key >= cd, cand) >= kf, cand, thr)

        thr = lax.fori_loop(0, 32, bisect, jnp.full((1, t), INT_MIN, jnp.int32))
        thr = jnp.maximum(thr, jnp.int32(INT_MIN + 1))
        n_ge = count(lambda key, pos, th: key >= th, thr)

        def tie_cut(_):
            need = kf - count(lambda key, pos, th: key > th, thr)
            def step(n, cut):
                cand = cut + lax.shift_left(jnp.int32(1), idx_bits - 1 - n)
                below = count(lambda key, pos, th, cd: (key == th) & (pos < cd), thr, cand)
                return jnp.where(below < need, cand, cut)
            return lax.fori_loop(0, idx_bits, step, jnp.zeros((1, t), jnp.int32))

        cut = lax.cond(jnp.max(n_ge) > kf, tie_cut, lambda _: jnp.full((1, t), 2 ** idx_bits - 1, jnp.int32), 0)

        def bias_body(c, _):
            key = key_ref[chunk(c), :]
            sel = (key > thr) | ((key == thr) & (c * t + row <= cut))
            bias_ref[chunk(c), :] = jnp.where(sel, 0.0, NEG_BIG)
            return 0

        lax.fori_loop(0, nk, bias_body, 0)

    q = q_ref[...]

    def logits(c):
        kc = k_ref[chunk(jnp.minimum(c, nk - 1)), :]
        return lax.dot_general(kc, q, (((1,), (1,)), ((), ())), preferred_element_type=F32)

    def weights(c, s, m, l):
        s = s + bias_ref[chunk(c), :]
        m_new = jnp.maximum(m, jnp.max(s, axis=0, keepdims=True))
        alpha = jnp.exp2(m - m_new)
        p = jnp.exp2(s - m_new)
        return p.astype(BF16), alpha, m_new, alpha * l + jnp.sum(p, axis=0, keepdims=True)

    def values(c, p, alpha, acc):
        pv = lax.dot_general(v_ref[chunk(c), :], p, (((0,), (0,)), ((), ())), preferred_element_type=F32)
        return alpha * acc + pv

    def step(n, st, slot):
        alpha, m, l, acc = st
        s_scr[1 - slot] = logits(n + 2)
        acc = values(n, p_scr[slot], alpha, acc)
        p, alpha, m, l = weights(n + 1, s_scr[slot], m, l)
        p_scr[1 - slot] = p
        return alpha, m, l, acc

    p0, alpha0, m0, l0 = weights(0, logits(0), jnp.full((1, t), NEG_BIG, F32), jnp.zeros((1, t), F32))
    p_scr[0] = p0
    s_scr[0] = logits(1)
    n_steps = nk - 1
    st = (alpha0, m0, l0, jnp.zeros((HEAD_DIM, t), F32))
    def two_steps(n, c):
        return step(n + 1, step(n, c, 0), 1)

    n_four = lax.shift_right_logical(n_steps, 2)
    st = lax.fori_loop(0, n_four, lambda qq, c: two_steps(4 * qq + 2, two_steps(4 * qq, c)), st)
    st = lax.fori_loop(2 * n_four, lax.shift_right_logical(n_steps, 1), lambda pp, c: two_steps(2 * pp, c), st)
    alpha, _, l, acc = lax.cond((n_steps & 1) == 1, lambda c: step(n_steps - 1, c, 0), lambda c: c, st)
    acc = values(nk - 1, p_scr[n_steps & 1], alpha, acc)
    o_ref[...] = (acc / l).T.astype(o_ref.dtype)


def _dsa_core(main, tail, n_heads, t=512):
    b, s, _ = main.shape
    dm = n_heads * HEAD_DIM
    topk = min(DSA_TOPK_MAX, s // 4)
    idx_bits = max(1, (s - 1).bit_length())
    qi_block = (3 * dm) // (IDX_HEADS * IDX_DIM)
    return pl.pallas_call(
        functools.partial(_dsa_core_kernel, t=t, topk=topk, idx_bits=idx_bits, sub=min(t, 64)),
        grid=(b, s // t, n_heads),
        in_specs=[
            pl.BlockSpec((None, t, HEAD_DIM), lambda b, i, h: (b, i, h)),
            pl.BlockSpec((None, s, HEAD_DIM), lambda b, i, h: (b, 0, n_heads + h)),
            pl.BlockSpec((None, s, HEAD_DIM), lambda b, i, h: (b, 0, 2 * n_heads + h)),
            pl.BlockSpec((None, t, IDX_HEADS * IDX_DIM), lambda b, i, h: (b, i, qi_block)),
            pl.BlockSpec((None, s, LANES), lambda b, i, h: (b, 0, 0)),
            pl.BlockSpec((None, t, LANES), lambda b, i, h: (b, i, 0)),
        ],
        out_specs=pl.BlockSpec((None, t, HEAD_DIM), lambda b, i, h: (b, i, h)),
        out_shape=jax.ShapeDtypeStruct((b, s, dm), BF16),
        scratch_shapes=[pltpu.VMEM((s, t), jnp.int32), pltpu.VMEM((s, t), F32),
                        pltpu.VMEM((2, t, t), F32), pltpu.VMEM((2, t, t), BF16)],
        compiler_params=_cparams(("parallel", "parallel", "arbitrary")),
        name="dsa_core",
    )(main, main, main, main, tail, tail)


def kernel(x, c, positions, ln1_g, ln2_g, ada_w, ada_b, mlp_w1, mlp_w2, sb_w_in, sb_q_gain, sb_k_gain, sb_w_out, s5_w_in, s5_lambda_re, s5_lambda_im, s5_log_dt, s5_b_re, s5_b_im, s5_c_re, s5_c_im, s5_d, s5_w_glu, dsa_w_in, dsa_q_gain, dsa_k_gain, dsa_w_out):
    depth = ada_w.shape[0]
    b, s, d = x.shape
    n_heads = d // HEAD_DIM
    scale = HEAD_DIM ** -0.5

    mod = _ada_mod(c, ada_w, ada_b)
    rope_tabs = _rope_tables(positions) if depth > 2 else None
    mlp_w1, mlp_w2, sb_w_in, sb_w_out, s5_w_in, s5_w_glu, dsa_w_out = (
        w.astype(BF16) for w in (mlp_w1, mlp_w2, sb_w_in, sb_w_out, s5_w_in, s5_w_glu, dsa_w_out))

    counts = [0, 0, 0]
    for i in range(depth):
        mod3 = mod[i].reshape(SUBLANES, 1, 6 * d)
        kind = i % N_MIXERS
        j = counts[kind]
        counts[kind] += 1
        if kind == 0:
            gain = jnp.concatenate([jnp.tile(sb_q_gain[j].astype(F32) * (scale * LOG2_E), n_heads),
                                    jnp.tile(sb_k_gain[j].astype(F32), n_heads),
                                    jnp.ones((d,), F32)]).reshape(1, 3 * d)
            qkv = _sb_inproj(x, ln1_g[i], mod3, sb_w_in, j, gain)
            o = _sb_attention(qkv, n_heads)
            x = _out_gate(o, sb_w_out, j, x, mod3, 2)
        elif kind == 1:
            u = _nm_matmul(x, ln1_g[i], mod3, s5_w_in, j, F32)
            tables = _s5_tables(s5_lambda_re[j], s5_lambda_im[j], s5_log_dt[j], s5_b_re[j], s5_b_im[j],
                                s5_c_re[j], s5_c_im[j])
            z = _s5_core(u, tables, s5_d[j])
            x = _glu_gate(z, s5_w_glu, j, x, mod3, 2)
        else:
            tn = 512
            n_in = dsa_w_in.shape[2]
            n_main = 3 * d + IDX_HEADS * IDX_DIM
            w_in = dsa_w_in[j].astype(BF16)
            w_tail = jnp.zeros((d, LANES), BF16).at[:, :n_in - n_main].set(w_in[:, n_main:])
            gain = jnp.zeros((1, n_main), F32)
            gain = gain.at[0, :d].set(jnp.tile(dsa_q_gain[j].astype(F32) * (scale * LOG2_E), n_heads))
            gain = gain.at[0, d:2 * d].set(jnp.tile(dsa_k_gain[j].astype(F32), n_heads))
            main, tail = _dsa_inproj(x, ln1_g[i], mod3, w_in[:, :n_main], w_tail, gain, rope_tabs, n_heads, tn=tn)
            o = _dsa_core(main, tail, n_heads)
            x = _out_gate(o, dsa_w_out, j, x, mod3, 2)
        x = _mlp(x, ln2_g[i], mod3, mlp_w1, mlp_w2, i)
    return x
```

```python
import functools
import math

import jax
import jax.numpy as jnp
from jax import lax
from jax.experimental import pallas as pl
from jax.experimental.pallas import tpu as pltpu

F32 = jnp.float32
BF16 = jnp.bfloat16

N_MIXERS = 3
HEAD_DIM = 128
ROPE_THETA = 10000.0
EPS = 1e-6
S5_GROUP = 16
S5_STATE = 64
IDX_HEADS = 16
IDX_DIM = 64
DSA_TOPK_MAX = 256

LANES = 128
SUBLANES = 8
ROW_STRIP = 16
VMEM_LIMIT_BYTES = 56 * 1024 * 1024

S5_CHUNK = 16
NEG_BIG = -1e30
INT_MIN = -(2 ** 31)
LOG2_E = 1.4426950408889634


def _cparams(sem):
    return pltpu.CompilerParams(dimension_semantics=sem, vmem_limit_bytes=VMEM_LIMIT_BYTES)


def _sigmoid(x):
    return 1.0 / (1.0 + jnp.exp(-x))


def _gelu_tanh(x):
    c = math.sqrt(2.0 / math.pi)
    return 0.5 * x * (1.0 + jnp.tanh(c * (x + 0.044715 * (x * x * x))))


def _rms_mod_store(h_ref, x_ref, g_ref, sh_ref, sc_ref):
    d = x_ref.shape[1]
    gain = jnp.broadcast_to(g_ref[...] * (1.0 + sc_ref[...]), (ROW_STRIP, d))
    shift = jnp.broadcast_to(sh_ref[...], (ROW_STRIP, d))

    def body(r, _):
        rows = pl.ds(pl.multiple_of(r * ROW_STRIP, ROW_STRIP), ROW_STRIP)
        x = x_ref[rows, :]
        ms = jnp.mean(x * x, axis=-1, keepdims=True)
        h_ref[rows, :] = (x * lax.rsqrt(ms + EPS) * gain + shift).astype(h_ref.dtype)
        return 0

    lax.fori_loop(0, x_ref.shape[0] // ROW_STRIP, body, 0, unroll=8)


def _head_rmsnorm(y, gain):
    ms = jnp.mean(y * y, axis=-1, keepdims=True)
    return y * lax.rsqrt(ms + EPS) * gain


def _ada_kernel(c_ref, w_ref, b_ref, o_ref):
    c = c_ref[...]
    cond = (c * _sigmoid(c)).astype(BF16)
    o_ref[...] = jnp.dot(cond, w_ref[...].astype(BF16), preferred_element_type=F32) + b_ref[...]


def _ada_mod(c, ada_w, ada_b):
    depth, d, n = ada_w.shape
    b = c.shape[0]
    c8 = jnp.zeros((SUBLANES, d), F32).at[:b].set(c)
    tn = 1024
    return pl.pallas_call(
        _ada_kernel,
        grid=(depth, n // tn),
        in_specs=[
            pl.BlockSpec((SUBLANES, d), lambda l, j: (0, 0)),
            pl.BlockSpec((None, d, tn), lambda l, j: (l, 0, j)),
            pl.BlockSpec((None, 1, tn), lambda l, j: (l, 0, j)),
        ],
        out_specs=pl.BlockSpec((None, SUBLANES, tn), lambda l, j: (l, 0, j)),
        out_shape=jax.ShapeDtypeStruct((depth, SUBLANES, n), F32),
        compiler_params=_cparams(("parallel", "parallel")),
        name="ada_mod",
    )(c8, ada_w, ada_b.reshape(depth, 1, n))


def _mod_specs(d, shift_chunk, scale_chunk):
    return [
        pl.BlockSpec((None, 1, d), lambda b, i, j: (b, 0, shift_chunk)),
        pl.BlockSpec((None, 1, d), lambda b, i, j: (b, 0, scale_chunk)),
    ]


def _nm_matmul_kernel(x_ref, g_ref, sh_ref, sc_ref, w_ref, o_ref, h_ref):
    @pl.when(pl.program_id(2) == 0)
    def _():
        _rms_mod_store(h_ref, x_ref, g_ref, sh_ref, sc_ref)

    o_ref[...] = jnp.dot(h_ref[...], w_ref[...], preferred_element_type=F32).astype(o_ref.dtype)


def _nm_matmul(x, ln_g, mod3, w, layer, out_dtype, tm=1024, tn=1024):
    b, s, d = x.shape
    n = w.shape[2]
    return pl.pallas_call(
        _nm_matmul_kernel,
        grid=(b, s // tm, n // tn),
        in_specs=[
            pl.BlockSpec((None, tm, d), lambda b, i, j: (b, i, 0)),
            pl.BlockSpec((1, d), lambda b, i, j: (0, 0)),
            *_mod_specs(d, 0, 1),
            pl.BlockSpec((None, d, tn), lambda b, i, j: (layer, 0, j)),
        ],
        out_specs=pl.BlockSpec((None, tm, tn), lambda b, i, j: (b, i, j)),
        out_shape=jax.ShapeDtypeStruct((b, s, n), out_dtype),
        scratch_shapes=[pltpu.VMEM((tm, d), BF16)],
        compiler_params=_cparams(("parallel", "parallel", "arbitrary")),
        name="nm_matmul",
    )(x, ln_g.reshape(1, d), mod3, mod3, w)


def _sb_inproj_kernel(x_ref, g_ref, sh_ref, sc_ref, w_ref, gain_ref, o_ref, h_ref, *, n_norm_tiles):
    j = pl.program_id(2)

    @pl.when(j == 0)
    def _():
        _rms_mod_store(h_ref, x_ref, g_ref, sh_ref, sc_ref)

    y = jnp.dot(h_ref[...], w_ref[...], preferred_element_type=F32)

    @pl.when(j < n_norm_tiles)
    def _():
        for hh in range(y.shape[1] // HEAD_DIM):
            sl = slice(hh * HEAD_DIM, (hh + 1) * HEAD_DIM)
            o_ref[:, sl] = _head_rmsnorm(y[:, sl], gain_ref[:, sl]).astype(o_ref.dtype)

    @pl.when(j >= n_norm_tiles)
    def _():
        o_ref[...] = y.astype(o_ref.dtype)


def _sb_inproj(x, ln_g, mod3, w, layer, gain_row, tm=1024, tn=1024):
    b, s, d = x.shape
    n = w.shape[2]
    return pl.pallas_call(
        functools.partial(_sb_inproj_kernel, n_norm_tiles=2 * d // tn),
        grid=(b, s // tm, n // tn),
        in_specs=[
            pl.BlockSpec((None, tm, d), lambda b, i, j: (b, i, 0)),
            pl.BlockSpec((1, d), lambda b, i, j: (0, 0)),
            *_mod_specs(d, 0, 1),
            pl.BlockSpec((None, d, tn), lambda b, i, j: (layer, 0, j)),
            pl.BlockSpec((1, tn), lambda b, i, j: (0, j)),
        ],
        out_specs=pl.BlockSpec((None, tm, tn), lambda b, i, j: (b, i, j)),
        out_shape=jax.ShapeDtypeStruct((b, s, n), BF16),
        scratch_shapes=[pltpu.VMEM((tm, d), BF16)],
        compiler_params=_cparams(("parallel", "parallel", "arbitrary")),
        name="sb_inproj",
    )(x, ln_g.reshape(1, d), mod3, mod3, w, gain_row)


def _out_gate_kernel(a_ref, w_ref, x_ref, gate_ref, o_ref):
    y = jnp.dot(a_ref[...], w_ref[...], preferred_element_type=F32)
    o_ref[...] = x_ref[...] + gate_ref[...] * y


def _out_gate(a, w, layer, x, mod3, gate_chunk, tm=1024, tn=1024):
    b, s, d = x.shape
    k = a.shape[2]
    nt = d // tn
    return pl.pallas_call(
        _out_gate_kernel,
        grid=(b, s // tm, nt),
        in_specs=[
            pl.BlockSpec((None, tm, k), lambda b, i, j: (b, i, 0)),
            pl.BlockSpec((None, k, tn), lambda b, i, j: (layer, 0, j)),
            pl.BlockSpec((None, tm, tn), lambda b, i, j: (b, i, j)),
            pl.BlockSpec((None, 1, tn), lambda b, i, j: (b, 0, gate_chunk * nt + j)),
        ],
        out_specs=pl.BlockSpec((None, tm, tn), lambda b, i, j: (b, i, j)),
        out_shape=jax.ShapeDtypeStruct((b, s, d), F32),
        compiler_params=_cparams(("parallel", "parallel", "parallel")),
        name="out_gate",
    )(a, w, x, mod3)


def _glu_gate_kernel(z_ref, wa_ref, wg_ref, x_ref, gate_ref, o_ref, zb_ref):
    @pl.when(pl.program_id(2) == 0)
    def _():
        zb_ref[...] = z_ref[...].astype(BF16)

    z = zb_ref[...]
    a = jnp.dot(z, wa_ref[...], preferred_element_type=F32)
    g = jnp.dot(z, wg_ref[...], preferred_element_type=F32)
    o_ref[...] = x_ref[...] + gate_ref[...] * (a * _sigmoid(g))


def _glu_gate(z, w_glu, layer, x, mod3, gate_chunk, tm=1024, tn=512):
    b, s, d = x.shape
    nt = d // tn
    return pl.pallas_call(
        _glu_gate_kernel,
        grid=(b, s // tm, nt),
        in_specs=[
            pl.BlockSpec((None, tm, d), lambda b, i, j: (b, i, 0)),
            pl.BlockSpec((None, d, tn), lambda b, i, j: (layer, 0, j)),
            pl.BlockSpec((None, d, tn), lambda b, i, j: (layer, 0, nt + j)),
            pl.BlockSpec((None, tm, tn), lambda b, i, j: (b, i, j)),
            pl.BlockSpec((None, 1, tn), lambda b, i, j: (b, 0, gate_chunk * nt + j)),
        ],
        out_specs=pl.BlockSpec((None, tm, tn), lambda b, i, j: (b, i, j)),
        out_shape=jax.ShapeDtypeStruct((b, s, d), F32),
        scratch_shapes=[pltpu.VMEM((tm, d), BF16)],
        compiler_params=_cparams(("parallel", "parallel", "arbitrary")),
        name="glu_gate",
    )(z, w_glu, w_glu, x, mod3)


def _mlp_kernel(x_ref, g_ref, sh_ref, sc_ref, w1_ref, w2_ref, gate_ref, o_ref, h_ref, acc_ref):
    j = pl.program_id(2)

    @pl.when(j == 0)
    def _():
        _rms_mod_store(h_ref, x_ref, g_ref, sh_ref, sc_ref)
        acc_ref[...] = jnp.zeros_like(acc_ref)

    a = jnp.maximum(jnp.dot(h_ref[...], w1_ref[...], preferred_element_type=F32), 0.0)
    acc_ref[...] += jnp.dot((a * a).astype(BF16), w2_ref[...], preferred_element_type=F32)

    @pl.when(j == pl.num_programs(2) - 1)
    def _():
        o_ref[...] = x_ref[...] + gate_ref[...] * acc_ref[...]


def _mlp(x, ln_g, mod3, w1, w2, layer, tm=512, tf=1024):
    b, s, d = x.shape
    f = w1.shape[2]
    return pl.pallas_call(
        _mlp_kernel,
        grid=(b, s // tm, f // tf),
        in_specs=[
            pl.BlockSpec((None, tm, d), lambda b, i, j: (b, i, 0)),
            pl.BlockSpec((1, d), lambda b, i, j: (0, 0)),
            *_mod_specs(d, 3, 4),
            pl.BlockSpec((None, d, tf), lambda b, i, j: (layer, 0, j)),
            pl.BlockSpec((None, tf, d), lambda b, i, j: (layer, j, 0)),
            pl.BlockSpec((None, 1, d), lambda b, i, j: (b, 0, 5)),
        ],
        out_specs=pl.BlockSpec((None, tm, d), lambda b, i, j: (b, i, 0)),
        out_shape=jax.ShapeDtypeStruct((b, s, d), F32),
        scratch_shapes=[pltpu.VMEM((tm, d), BF16), pltpu.VMEM((tm, d), F32)],
        compiler_params=_cparams(("parallel", "parallel", "arbitrary")),
        name="mlp",
    )(x, ln_g.reshape(1, d), mod3, mod3, w1, w2, mod3)


def _sb_attn_kernel(q_ref, k_ref, v_ref, later_ref, o_ref, z_scr, e_scr, *, tq, tk, hps):
    i = pl.program_id(2)
    m = (i + 1) * (tq // tk)
    n_diag = tq // tk
    heads = range(hps)
    cols = [slice(hh * HEAD_DIM, (hh + 1) * HEAD_DIM) for hh in heads]
    qs = [q_ref[:, cols[hh]] for hh in heads]
    lead = lax.broadcasted_iota(jnp.int32, (tq, tk), 0) - lax.broadcasted_iota(jnp.int32, (tq, tk), 1)
    sign = jnp.uint32(0x80000000)

    def key_start(n):
        return pl.multiple_of(jnp.maximum(m - 1 - n, 0) * tk, tk)

    def score(hh, n):
        kb = k_ref[pl.ds(key_start(n), tk), cols[hh]]
        return lax.dot_general(qs[hh], kb, (((1,), (1,)), ((), ())), preferred_element_type=F32)

    def log_weights(n, z, masked, lead=lead):
        neg_abs = pltpu.bitcast(pltpu.bitcast(z, jnp.uint32) | sign, F32)
        ls = jnp.minimum(z, 0.0) - jnp.log(1.0 + jnp.exp2(neg_abs)) * LOG2_E
        ln = ls - z
        if masked:
            causal = key_start(n) - i * tq < lead
            ln = jnp.where(causal, ln, 0.0)
            ls = jnp.where(causal, ls, NEG_BIG)
        ln = ln.astype(BF16)
        later = jnp.dot(ln, later_ref[...], preferred_element_type=F32)
        return ls + later, later[:, 0:1] + ln[:, 0:1].astype(F32)

    def accumulate(hh, n, e, carry, acc):
        vb = v_ref[pl.ds(key_start(n), tk), cols[hh]]
        w = jnp.exp2(e + carry).astype(BF16)
        return acc + jnp.dot(w, vb, preferred_element_type=F32)

    states = []
    top = tq - tk
    for hh in heads:
        kb = k_ref[pl.ds(key_start(0), tk), cols[hh]]
        z0 = lax.dot_general(qs[hh][top:], kb, (((1,), (1,)), ((), ())), preferred_element_type=F32)
        e0, tot0 = log_weights(0, z0, True, lead[top:])
        e0 = jnp.concatenate([jnp.full((top, tk), NEG_BIG, F32), e0], axis=0)
        tot0 = jnp.concatenate([jnp.zeros((top, 1), F32), tot0], axis=0)
        e_scr[hh, 0] = e0
        z_scr[hh, 0] = score(hh, 1)
        states.append((jnp.zeros((tq, 1), F32), tot0, jnp.zeros((tq, HEAD_DIM), F32)))

    def step(n, sts):
        slot = n % 2
        out = []
        for hh in heads:
            carry, tot, acc = sts[hh]
            z_scr[hh, 1 - slot] = score(hh, n + 2)
            acc = accumulate(hh, n, e_scr[hh, slot], carry, acc)
            e_next, tot_next = log_weights(n + 1, z_scr[hh, slot], True)
            e_scr[hh, 1 - slot] = e_next
            out.append((carry + tot, tot_next, acc))
        return out

    def pair(p, sts):
        n = n_diag - 1 + 2 * p
        z_mid = [score(hh, n + 2) for hh in heads]
        accs = [accumulate(hh, n, e_scr[hh, 1], sts[hh][0], sts[hh][2]) for hh in heads]
        mids = [log_weights(n + 1, z_scr[hh, 1], False) for hh in heads]
        carries = [sts[hh][0] + sts[hh][1] for hh in heads]
        for hh in heads:
            z_scr[hh, 1] = score(hh, n + 3)
        accs = [accumulate(hh, n + 1, mids[hh][0], carries[hh], accs[hh]) for hh in heads]
        lasts = [log_weights(n + 2, z_mid[hh], False) for hh in heads]
        for hh in heads:
            e_scr[hh, 1] = lasts[hh][0]
        return tuple((carries[hh] + mids[hh][1], lasts[hh][1], accs[hh]) for hh in heads)

    assert n_diag % 2 == 0
    for n in range(n_diag - 1):
        states = step(n, states)
    n_pairs = (i * n_diag) // 2
    n_quads = lax.shift_right_logical(n_pairs, 1)
    states = lax.fori_loop(0, n_quads, lambda qd, c: pair(2 * qd + 1, pair(2 * qd, c)), tuple(states))
    states = lax.fori_loop(2 * n_quads, n_pairs, pair, states)
    for hh in heads:
        carry, _, acc = states[hh]
        o_ref[:, cols[hh]] = accumulate(hh, m - 1, e_scr[hh, 1], carry, acc).astype(o_ref.dtype)


def _sb_attention(qkv, n_heads, tq=512, tk=256, hps=1):
    b, s, _ = qkv.shape
    groups = n_heads // hps
    width = hps * HEAD_DIM
    pos = jnp.arange(tk)
    later_mat = (pos[:, None] > pos[None, :]).astype(BF16)
    return pl.pallas_call(
        functools.partial(_sb_attn_kernel, tq=tq, tk=tk, hps=hps),
        grid=(b, groups, s // tq),
        in_specs=[
            pl.BlockSpec((None, tq, width), lambda b, g, i: (b, i, g)),
            pl.BlockSpec((None, s, width), lambda b, g, i: (b, 0, groups + g)),
            pl.BlockSpec((None, s, width), lambda b, g, i: (b, 0, 2 * groups + g)),
            pl.BlockSpec((tk, tk), lambda b, g, i: (0, 0)),
        ],
        out_specs=pl.BlockSpec((None, tq, width), lambda b, g, i: (b, i, g)),
        out_shape=jax.ShapeDtypeStruct((b, s, n_heads * HEAD_DIM), BF16),
        scratch_shapes=[pltpu.VMEM((hps, 2, tq, tk), F32), pltpu.VMEM((hps, 2, tq, tk), F32)],
        compiler_params=_cparams(("parallel", "parallel", "parallel")),
        name="sb_attn",
    )(qkv, qkv, qkv, later_mat)


def _s5_tables(lam_re, lam_im, log_dt, b_re, b_im, c_re, c_im):
    g, p, gc = b_re.shape
    l = S5_CHUNK
    gpt = LANES // gc
    nt = g // gpt
    hi = lax.Precision.HIGHEST
    dt = jnp.exp(log_dt.astype(F32))[:, None]
    lr = lam_re.astype(F32)
    li = lam_im.astype(F32)
    mag = jnp.exp(lr * dt)
    ar = mag * jnp.cos(li * dt)
    ai = mag * jnp.sin(li * dt)
    den = lr * lr + li * li
    fr = ((ar - 1.0) * lr + ai * li) / den
    fi = (ai * lr - (ar - 1.0) * li) / den
    br_ = b_re.astype(F32)
    bi_ = b_im.astype(F32)
    bbr = fr[..., None] * br_ - fi[..., None] * bi_
    bbi = fr[..., None] * bi_ + fi[..., None] * br_
    cr = c_re.astype(F32)
    ci = c_im.astype(F32)
    n = jnp.arange(l + 1, dtype=F32)[:, None, None]
    pw_r = jnp.exp(n * (lr * dt)) * jnp.cos(n * (li * dt))
    pw_i = jnp.exp(n * (lr * dt)) * jnp.sin(n * (li * dt))
    bt_r = bbr.transpose(0, 2, 1)
    bt_i = bbi.transpose(0, 2, 1)
    ab_r = pw_r[:l, :, None, :] * bt_r - pw_i[:l, :, None, :] * bt_i
    ab_i = pw_r[:l, :, None, :] * bt_i + pw_i[:l, :, None, :] * bt_r

    def tile_rows(t):
        cols = t.shape[-1]
        return t.reshape(l, nt, gpt, gc, cols).transpose(1, 0, 2, 3, 4).reshape(nt, l * LANES, cols)

    wz = tile_rows(jnp.concatenate([ab_r[::-1], ab_i[::-1]], axis=-1))

    kern = (jnp.einsum('gcp,ngdp->gdnc', cr, ab_r, precision=hi)
            - jnp.einsum('gcp,ngdp->gdnc', ci, ab_i, precision=hi)).reshape(g, gc, l * gc)
    intra = tile_rows(jnp.stack([jnp.pad(kern[:, :, :(l - j) * gc], ((0, 0), (0, 0), (j * gc, 0)))
                                 for j in range(l)]))

    ct_r = jnp.tile(cr.transpose(0, 2, 1), (1, 1, l))
    ct_i = jnp.tile(ci.transpose(0, 2, 1), (1, 1, l))
    p1_r = jnp.repeat(pw_r[1:].transpose(1, 2, 0), gc, axis=-1)
    p1_i = jnp.repeat(pw_i[1:].transpose(1, 2, 0), gc, axis=-1)
    vr = (ct_r * p1_r - ct_i * p1_i).reshape(nt, gpt * p, l * gc)
    vi = (-(ct_r * p1_i + ct_i * p1_r)).reshape(nt, gpt * p, l * gc)
    wy = jnp.concatenate([intra, vr, vi], axis=1)

    alr = pw_r[l].reshape(nt, 1, gpt * p)
    ali = pw_i[l].reshape(nt, 1, gpt * p)
    return wz.astype(BF16), wy.astype(BF16), alr, ali


def _spread_groups(src_ref, dst_ref, row_group_div, col_unit):
    n_src = src_ref.shape[1]
    n_dst = dst_ref.shape[1]
    groups = n_dst // n_src
    rc = 256
    sr = lax.broadcasted_iota(jnp.int32, (n_src, n_dst), 0)
    dc = lax.broadcasted_iota(jnp.int32, (n_src, n_dst), 1)
    spread = ((sr // col_unit == dc // (col_unit * groups)) & (sr % col_unit == dc % col_unit))
    spread = jnp.where(spread, 1.0, 0.0).astype(BF16)
    rr = lax.broadcasted_iota(jnp.int32, (rc, n_dst), 0)
    cg = (lax.broadcasted_iota(jnp.int32, (rc, n_dst), 1) // col_unit) % groups
    for r0 in range(0, src_ref.shape[0], rc):
        wide = jnp.dot(src_ref[r0:r0 + rc, :], spread, preferred_element_type=F32)
        keep = ((r0 + rr) // row_group_div) % groups == cg
        dst_ref[r0:r0 + rc, :] = jnp.where(keep, wide, 0.0).astype(dst_ref.dtype)


def _s5_core_kernel(ut_ref, wzc_ref, wyc_ref, alr_ref, ali_ref, d_ref, o_ref,
                    wz_ref, wy_ref, zr_ref, zi_ref, hr_ref, hi_ref, *, l, nk, gc, p):
    @pl.when(pl.program_id(1) == 0)
    def _():
        n_in = l * LANES
        _spread_groups(wzc_ref, wz_ref, gc, p)
        _spread_groups(wyc_ref.at[:n_in], wy_ref.at[:n_in], gc, gc)
        _spread_groups(wyc_ref.at[n_in:], wy_ref.at[n_in:], p, gc)

    us = [ut_ref[pl.ds(j, nk, stride=l), :] for j in range(l)]
    ub = jnp.concatenate([u.astype(BF16) for u in us], axis=1)
    z = jnp.dot(ub, wz_ref[...], preferred_element_type=F32)
    half = z.shape[1] // 2
    zr_ref[...] = z[:, :half]
    zi_ref[...] = z[:, half:]
    alr = alr_ref[...]
    ali = ali_ref[...]
    rows = lax.broadcasted_iota(jnp.int32, (SUBLANES, half), 0)

    def body(kb, c):
        hr, hi = c
        base = pl.multiple_of(kb * SUBLANES, SUBLANES)
        zr = zr_ref[pl.ds(base, SUBLANES), :]
        zi = zi_ref[pl.ds(base, SUBLANES), :]
        out_r = jnp.zeros((SUBLANES, half), F32)
        out_i = jnp.zeros((SUBLANES, half), F32)
        for r in range(SUBLANES):
            out_r = jnp.where(rows == r, hr, out_r)
            out_i = jnp.where(rows == r, hi, out_i)
            hr, hi = (alr * hr - ali * hi + zr[r:r + 1, :], alr * hi + ali * hr + zi[r:r + 1, :])
        hr_ref[pl.ds(base, SUBLANES), :] = out_r
        hi_ref[pl.ds(base, SUBLANES), :] = out_i
        return hr, hi

    zero = jnp.zeros((1, half), F32)
    lax.fori_loop(0, nk // SUBLANES, body, (zero, zero))

    lhs = jnp.concatenate([ub, hr_ref[...].astype(BF16), hi_ref[...].astype(BF16)], axis=1)
    y = jnp.dot(lhs, wy_ref[...], preferred_element_type=F32)
    d = d_ref[...]
    for i in range(l):
        yi = y[:, i * LANES:(i + 1) * LANES] + d * us[i]
        o_ref[pl.ds(i, nk, stride=l), :] = _gelu_tanh(yi).astype(o_ref.dtype)


def _s5_core(u, tables, d_skip):
    b, s, d = u.shape
    l = S5_CHUNK
    nk = s // l
    nt = d // LANES
    wz, wy, alr, ali = tables
    half = alr.shape[2]
    gpt = LANES // S5_GROUP
    return pl.pallas_call(
        functools.partial(_s5_core_kernel, l=l, nk=nk, gc=S5_GROUP, p=half // gpt),
        grid=(nt, b),
        in_specs=[
            pl.BlockSpec((None, s, LANES), lambda t, b: (b, 0, t)),
            pl.BlockSpec((None,) + wz.shape[1:], lambda t, b: (t, 0, 0)),
            pl.BlockSpec((None,) + wy.shape[1:], lambda t, b: (t, 0, 0)),
            pl.BlockSpec((None, 1, half), lambda t, b: (t, 0, 0)),
            pl.BlockSpec((None, 1, half), lambda t, b: (t, 0, 0)),
            pl.BlockSpec((None, 1, LANES), lambda t, b: (t, 0, 0)),
        ],
        out_specs=pl.BlockSpec((None, s, LANES), lambda t, b: (b, 0, t)),
        out_shape=jax.ShapeDtypeStruct((b, s, d), F32),
        scratch_shapes=[pltpu.VMEM((l * LANES, 2 * half), BF16),
                        pltpu.VMEM((l * LANES + 2 * half, l * LANES), BF16)] + [pltpu.VMEM((nk, half), F32)] * 4,
        compiler_params=_cparams(("arbitrary", "arbitrary")),
        name="s5_core",
    )(u, wz, wy, alr, ali, d_skip.astype(F32).reshape(nt, 1, LANES))


def _rope_tab_kernel(pos_ref, f_head_ref, f_idx_ref, ch_ref, sh_ref, ci_ref, si_ref):
    p = pos_ref[...].astype(F32)
    lane = lax.broadcasted_iota(jnp.int32, ch_ref.shape, 1)
    a = p * f_head_ref[...]
    ch_ref[...] = jnp.cos(a)
    sh_ref[...] = jnp.where(lane < HEAD_DIM // 2, -jnp.sin(a), jnp.sin(a))
    a = p * f_idx_ref[...]
    ci_ref[...] = jnp.cos(a)
    si_ref[...] = jnp.where(lane % IDX_DIM < IDX_DIM // 2, -jnp.sin(a), jnp.sin(a))


def _rope_tables(positions, tm=1024):
    b, s = positions.shape

    def inv_freq(dim):
        return ROPE_THETA ** (-jnp.arange(0, dim, 2, dtype=F32) / dim)

    f_head = jnp.tile(inv_freq(HEAD_DIM), 2).reshape(1, LANES)
    f_idx = jnp.tile(inv_freq(IDX_DIM), 2 * LANES // IDX_DIM).reshape(1, LANES)
    row_spec = pl.BlockSpec((None, tm, LANES), lambda b, i: (b, i, 0))
    return pl.pallas_call(
        _rope_tab_kernel,
        grid=(b, s // tm),
        in_specs=[
            pl.BlockSpec((None, tm, 1), lambda b, i: (b, i, 0)),
            pl.BlockSpec((1, LANES), lambda b, i: (0, 0)),
            pl.BlockSpec((1, LANES), lambda b, i: (0, 0)),
        ],
        out_specs=[row_spec] * 4,
        out_shape=[jax.ShapeDtypeStruct((b, s, LANES), F32)] * 4,
        compiler_params=_cparams(("parallel", "parallel")),
        name="rope_tables",
    )(positions.reshape(b, s, 1), f_head, f_idx)


def _rope_head(y, cos, sin_signed):
    return y * cos + pltpu.roll(y, HEAD_DIM // 2, axis=1) * sin_signed


def _rope_idx(y, cos, sin_signed):
    lane = lax.broadcasted_iota(jnp.int32, y.shape, 1)
    half = IDX_DIM // 2
    partner = jnp.where(lane % IDX_DIM < half, pltpu.roll(y, LANES - half, axis=1), pltpu.roll(y, half, axis=1))
    return y * cos + partner * sin_signed


def _dsa_inproj_kernel(x_ref, g_ref, sh_ref, sc_ref, w_ref, wt_ref, gain_ref, ch_ref, shd_ref, ci_ref, si_ref,
                       o_ref, tail_ref, h_ref, *, n_qk, n_v, n_qi):
    j = pl.program_id(2)

    @pl.when(j == 0)
    def _():
        _rms_mod_store(h_ref, x_ref, g_ref, sh_ref, sc_ref)

    y = jnp.dot(h_ref[...], w_ref[...], preferred_element_type=F32)
    slices = [slice(hh * LANES, (hh + 1) * LANES) for hh in range(y.shape[1] // LANES)]

    @pl.when(j < n_qk)
    def _():
        for sl in slices:
            yn = _head_rmsnorm(y[:, sl], gain_ref[:, sl])
            o_ref[:, sl] = _rope_head(yn, ch_ref[...], shd_ref[...]).astype(o_ref.dtype)

    @pl.when((j >= n_qk) & (j < n_qk + n_v))
    def _():
        o_ref[...] = y.astype(o_ref.dtype)

    @pl.when((j >= n_qk + n_v) & (j < n_qk + n_v + n_qi))
    def _():
        for sl in slices:
            o_ref[:, sl] = (_rope_idx(y[:, sl], ci_ref[...], si_ref[...]) * IDX_DIM ** -0.5).astype(o_ref.dtype)

    @pl.when(j == n_qk + n_v + n_qi - 1)
    def _():
        t = jnp.dot(h_ref[...], wt_ref[...], preferred_element_type=F32)
        lane = lax.broadcasted_iota(jnp.int32, t.shape, 1)
        tail_ref[...] = jnp.where(lane < IDX_DIM, _rope_idx(t, ci_ref[...], si_ref[...]), t * IDX_HEADS ** -0.5)


def _dsa_inproj(x, ln_g, mod3, w_main, w_tail, gain_row, tabs, n_heads, tm=1024, tn=512):
    b, s, d = x.shape
    dm = n_heads * HEAD_DIM
    n_qk, n_v, n_qi = 2 * dm // tn, dm // tn, IDX_HEADS * IDX_DIM // tn
    n_main = n_qk + n_v + n_qi
    tab_spec = pl.BlockSpec((None, tm, LANES), lambda b, i, j: (b, i, 0))
    return pl.pallas_call(
        functools.partial(_dsa_inproj_kernel, n_qk=n_qk, n_v=n_v, n_qi=n_qi),
        grid=(b, s // tm, n_main),
        in_specs=[
            pl.BlockSpec((None, tm, d), lambda b, i, j: (b, i, 0)),
            pl.BlockSpec((1, d), lambda b, i, j: (0, 0)),
            *_mod_specs(d, 0, 1),
            pl.BlockSpec((d, tn), lambda b, i, j: (0, j)),
            pl.BlockSpec((d, LANES), lambda b, i, j: (0, 0)),
            pl.BlockSpec((1, tn), lambda b, i, j: (0, j)),
            tab_spec, tab_spec, tab_spec, tab_spec,
        ],
        out_specs=[
            pl.BlockSpec((None, tm, tn), lambda b, i, j: (b, i, j)),
            pl.BlockSpec((None, tm, LANES), lambda b, i, j: (b, i, 0)),
        ],
        out_shape=[jax.ShapeDtypeStruct((b, s, n_main * tn), BF16),
                   jax.ShapeDtypeStruct((b, s, LANES), F32)],
        scratch_shapes=[pltpu.VMEM((tm, d), BF16)],
        compiler_params=_cparams(("parallel", "parallel", "arbitrary")),
        name="dsa_inproj",
    )(x, ln_g.reshape(1, d), mod3, mod3, w_main, w_tail, gain_row, *tabs)


def _dsa_core_kernel(q_ref, k_ref, v_ref, qi_ref, ki_ref, wi_ref, o_ref, key_ref, bias_ref, s_scr, p_scr,
                     *, t, topk, idx_bits, sub):
    i = pl.program_id(1)
    h = pl.program_id(2)
    nk = i + 1
    kf = float(topk)

    def chunk(c):
        return pl.ds(pl.multiple_of(c * t, t), t)

    @pl.when(h == 0)
    def _():
        qi = qi_ref[...]
        wi_t = wi_ref[...].T
        row = lax.broadcasted_iota(jnp.int32, (t, t), 0)
        col = lax.broadcasted_iota(jnp.int32, (t, t), 1)

        def score_body(c, _):
            ki = ki_ref[chunk(c), :][:, :IDX_DIM].astype(BF16)
            score = jnp.zeros((t, t), F32)
            for hh in range(IDX_HEADS):
                rel = lax.dot_general(ki, qi[:, hh * IDX_DIM:(hh + 1) * IDX_DIM], (((1,), (1,)), ((), ())),
                                      preferred_element_type=F32)
                score = score + wi_t[IDX_DIM + hh:IDX_DIM + hh + 1, :] * jnp.maximum(rel, 0.0)
            bits = pltpu.bitcast(score + 0.0, jnp.int32)
            key = jnp.where(bits < 0, bits ^ jnp.int32(0x7FFFFFFF), bits)
            key = jnp.where((c - i) * t + row <= col, key, jnp.int32(INT_MIN))
            key_ref[chunk(c), :] = key
            return 0

        lax.fori_loop(0, nk, score_body, 0)

        sub_row = lax.broadcasted_iota(jnp.int32, (sub, t), 0)

        def count(pred, *query_args):
            def body(c, acc):
                for r in range(t // sub):
                    start = pl.multiple_of(c * t + r * sub, sub)
                    hit = pred(key_ref[pl.ds(start, sub), :], start + sub_row, *query_args)
                    ones = jnp.where(hit, 1.0, 0.0)
                    acc = acc + jnp.sum(ones.reshape(sub // SUBLANES, SUBLANES, t), axis=0)
                return acc
            acc = lax.fori_loop(0, nk, body, jnp.zeros((SUBLANES, t), F32))
            return jnp.sum(acc, axis=0, keepdims=True)

        def bisect(n, thr):
            cand = thr + lax.shift_left(jnp.int32(1), 31 - n)
            return jnp.where(count(lambda key, pos, cd: key >= cd, cand) >= kf, cand, thr)

        thr = lax.fori_loop(0, 32, bisect, jnp.full((1, t), INT_MIN, jnp.int32))
        thr = jnp.maximum(thr, jnp.int32(INT_MIN + 1))
        n_ge = count(lambda key, pos, th: key >= th, thr)

        def tie_cut(_):
            need = kf - count(lambda key, pos, th: key > th, thr)
            def step(n, cut):
                cand = cut + lax.shift_left(jnp.int32(1), idx_bits - 1 - n)
                below = count(lambda key, pos, th, cd: (key == th) & (pos < cd), thr, cand)
                return jnp.where(below < need, cand, cut)
            return lax.fori_loop(0, idx_bits, step, jnp.zeros((1, t), jnp.int32))

        cut = lax.cond(jnp.max(n_ge) > kf, tie_cut, lambda _: jnp.full((1, t), 2 ** idx_bits - 1, jnp.int32), 0)

        def bias_body(c, _):
            key = key_ref[chunk(c), :]
            sel = (key > thr) | ((key == thr) & (c * t + row <= cut))
            bias_ref[chunk(c), :] = jnp.where(sel, 0.0, NEG_BIG)
            return 0

        lax.fori_loop(0, nk, bias_body, 0)

    q = q_ref[...]

    def logits(c):
        kc = chunk(jnp.minimum(c, nk - 1))
        s = lax.dot_general(k_ref[kc, :], q, (((1,), (1,)), ((), ())), preferred_element_type=F32)
        s = s + bias_ref[kc, :]
        return s, jnp.max(s, axis=0, keepdims=True)

    def weights(s, s_max, m, l):
        m_new = jnp.maximum(m, s_max)
        alpha = jnp.exp2(m - m_new)
        p = jnp.exp2(s - m_new)
        return p.astype(BF16), alpha, m_new, alpha * l + jnp.sum(p, axis=0, keepdims=True)

    def values(c, p, alpha, acc):
        pv = lax.dot_general(v_ref[chunk(c), :], p, (((0,), (0,)), ((), ())), preferred_element_type=F32)
        return alpha * acc + pv

    def step(n, st, slot):
        alpha, m, l, acc, s_max = st
        s_scr[1 - slot], s_max_next = logits(n + 2)
        acc = values(n, p_scr[slot], alpha, acc)
        p, alpha, m, l = weights(s_scr[slot], s_max, m, l)
        p_scr[1 - slot] = p
        return alpha, m, l, acc, s_max_next

    s0, s_max0 = logits(0)
    p0, alpha0, m0, l0 = weights(s0, s_max0, jnp.full((1, t), NEG_BIG, F32), jnp.zeros((1, t), F32))
    p_scr[0] = p0
    s_scr[0], s_max1 = logits(1)
    n_steps = nk - 1
    st = (alpha0, m0, l0, jnp.zeros((HEAD_DIM, t), F32), s_max1)
    def two_steps(n, c):
        return step(n + 1, step(n, c, 0), 1)

    n_four = lax.shift_right_logical(n_steps, 2)
    st = lax.fori_loop(0, n_four, lambda qq, c: two_steps(4 * qq + 2, two_steps(4 * qq, c)), st)
    st = lax.fori_loop(2 * n_four, lax.shift_right_logical(n_steps, 1), lambda pp, c: two_steps(2 * pp, c), st)
    alpha, _, l, acc, _ = lax.cond((n_steps & 1) == 1, lambda c: step(n_steps - 1, c, 0), lambda c: c, st)
    acc = values(nk - 1, p_scr[n_steps & 1], alpha, acc)
    o_ref[...] = (acc / l).T.astype(o_ref.dtype)


def _dsa_core(main, tail, n_heads, t=512):
    b, s, _ = main.shape
    dm = n_heads * HEAD_DIM
    topk = min(DSA_TOPK_MAX, s // 4)
    idx_bits = max(1, (s - 1).bit_length())
    qi_block = (3 * dm) // (IDX_HEADS * IDX_DIM)
    return pl.pallas_call(
        functools.partial(_dsa_core_kernel, t=t, topk=topk, idx_bits=idx_bits, sub=min(t, 64)),
        grid=(b, s // t, n_heads),
        in_specs=[
            pl.BlockSpec((None, t, HEAD_DIM), lambda b, i, h: (b, i, h)),
            pl.BlockSpec((None, s, HEAD_DIM), lambda b, i, h: (b, 0, n_heads + h)),
            pl.BlockSpec((None, s, HEAD_DIM), lambda b, i, h: (b, 0, 2 * n_heads + h)),
            pl.BlockSpec((None, t, IDX_HEADS * IDX_DIM), lambda b, i, h: (b, i, qi_block)),
            pl.BlockSpec((None, s, LANES), lambda b, i, h: (b, 0, 0)),
            pl.BlockSpec((None, t, LANES), lambda b, i, h: (b, i, 0)),
        ],
        out_specs=pl.BlockSpec((None, t, HEAD_DIM), lambda b, i, h: (b, i, h)),
        out_shape=jax.ShapeDtypeStruct((b, s, dm), BF16),
        scratch_shapes=[pltpu.VMEM((s, t), jnp.int32), pltpu.VMEM((s, t), F32),
                        pltpu.VMEM((2, t, t), F32), pltpu.VMEM((2, t, t), BF16)],
        compiler_params=_cparams(("parallel", "parallel", "arbitrary")),
        name="dsa_core",
    )(main, main, main, main, tail, tail)


def kernel(x, c, positions, ln1_g, ln2_g, ada_w, ada_b, mlp_w1, mlp_w2, sb_w_in, sb_q_gain, sb_k_gain, sb_w_out, s5_w_in, s5_lambda_re, s5_lambda_im, s5_log_dt, s5_b_re, s5_b_im, s5_c_re, s5_c_im, s5_d, s5_w_glu, dsa_w_in, dsa_q_gain, dsa_k_gain, dsa_w_out):
    depth = ada_w.shape[0]
    b, s, d = x.shape
    n_heads = d // HEAD_DIM
    scale = HEAD_DIM ** -0.5

    mod = _ada_mod(c, ada_w, ada_b)
    rope_tabs = _rope_tables(positions) if depth > 2 else None
    mlp_w1, mlp_w2, sb_w_in, sb_w_out, s5_w_in, s5_w_glu, dsa_w_out = (
        w.astype(BF16) for w in (mlp_w1, mlp_w2, sb_w_in, sb_w_out, s5_w_in, s5_w_glu, dsa_w_out))

    counts = [0, 0, 0]
    for i in range(depth):
        mod3 = mod[i].reshape(SUBLANES, 1, 6 * d)
        kind = i % N_MIXERS
        j = counts[kind]
        counts[kind] += 1
        if kind == 0:
            gain = jnp.concatenate([jnp.tile(sb_q_gain[j].astype(F32) * (scale * LOG2_E), n_heads),
                                    jnp.tile(sb_k_gain[j].astype(F32), n_heads),
                                    jnp.ones((d,), F32)]).reshape(1, 3 * d)
            qkv = _sb_inproj(x, ln1_g[i], mod3, sb_w_in, j, gain)
            o = _sb_attention(qkv, n_heads)
            x = _out_gate(o, sb_w_out, j, x, mod3, 2)
        elif kind == 1:
            u = _nm_matmul(x, ln1_g[i], mod3, s5_w_in, j, F32)
            tables = _s5_tables(s5_lambda_re[j], s5_lambda_im[j], s5_log_dt[j], s5_b_re[j], s5_b_im[j],
                                s5_c_re[j], s5_c_im[j])
            z = _s5_core(u, tables, s5_d[j])
            x = _glu_gate(z, s5_w_glu, j, x, mod3, 2)
        else:
            tn = 512
            n_in = dsa_w_in.shape[2]
            n_main = 3 * d + IDX_HEADS * IDX_DIM
            w_in = dsa_w_in[j].astype(BF16)
            w_tail = jnp.zeros((d, LANES), BF16).at[:, :n_in - n_main].set(w_in[:, n_main:])
            gain = jnp.zeros((1, n_main), F32)
            gain = gain.at[0, :d].set(jnp.tile(dsa_q_gain[j].astype(F32) * (scale * LOG2_E), n_heads))
            gain = gain.at[0, d:2 * d].set(jnp.tile(dsa_k_gain[j].astype(F32), n_heads))
            main, tail = _dsa_inproj(x, ln1_g[i], mod3, w_in[:, :n_main], w_tail, gain, rope_tabs, n_heads, tn=tn)
            o = _dsa_core(main, tail, n_heads)
            x = _out_gate(o, dsa_w_out, j, x, mod3, 2)
        x = _mlp(x, ln2_g[i], mod3, mlp_w1, mlp_w2, i)
    return x
```
